```python
import jax, jax.numpy as jnp
from jax import lax
import numpy as np

D_MODEL = 1024
BATCH = 32
SEQ = 256
DEPTH = 4
DEC_BATCH = 4
DEC_SEQ = 1024
PAST_LEN = 256

GRID_W = 64
POS_BASE = 10000.0
LN_EPS = 1e-5
A_W = D_MODEL // 2
A_HEADS = 4
A_DH = A_W // A_HEADS
N_DIR = 2
CHUNK = 64
B_W = D_MODEL // 4
B_GROUPS = 4
B_GW = B_W // B_GROUPS
C_W = D_MODEL // 4
POOL_WINDOWS = (2, 4, 8, 16)
C_GROUPS = len(POOL_WINDOWS)
C_GW = C_W // C_GROUPS
N_BRANCH = 3
_SEG = (A_W, A_W, A_W, A_W, N_DIR * 2 * A_HEADS, B_W, C_W, N_BRANCH * D_MODEL)
IN_SPLITS = tuple(int(s) for s in np.cumsum(_SEG)[:-1])
W_IN_COLS = int(sum(_SEG))
N_GROUPS = 4
EXP_PER_GROUP = 4
N_EXPERTS = N_GROUPS * EXP_PER_GROUP
TOP_K_INNER = 2
D_EXPERT = D_MODEL // 2
DEEPNORM_ALPHA = (2 * DEPTH) ** 0.25
DEEPNORM_BETA = (8 * DEPTH) ** -0.25

kernel_name = "hybrid_mlstm_fourier_pool_hmoe_diffusion_step"

F32 = jnp.float32


def _layer_norm(x, g, b):
    xf = x.astype(F32)
    mu = jnp.mean(xf, axis=-1, keepdims=True)
    var = jnp.mean(jnp.square(xf - mu), axis=-1, keepdims=True)
    y = (xf - mu) * lax.rsqrt(var + LN_EPS)
    return (y * g.astype(F32) + b.astype(F32)).astype(x.dtype)


def _grid_pos_embed(n_tok, dtype):
    rows = n_tok // GRID_W
    r, col = jnp.meshgrid(jnp.arange(rows, dtype=F32), jnp.arange(GRID_W, dtype=F32), indexing="ij")
    r = r.reshape(-1)
    col = col.reshape(-1)
    quarter = D_MODEL // 4
    freq = 1.0 / (POS_BASE ** (jnp.arange(quarter, dtype=F32) / quarter))
    ang_r = r[:, None] * freq[None, :]
    ang_c = col[:, None] * freq[None, :]
    emb = jnp.concatenate([jnp.sin(ang_r), jnp.cos(ang_r), jnp.sin(ang_c), jnp.cos(ang_c)], axis=-1)
    return emb.astype(dtype)


def _mlstm_scan(q, k, v, ig, lf, C0, n0, m0):
    Bn, H, T, DH = q.shape
    nc = T // CHUNK

    def chunks(a):
        a = a.reshape((Bn, H, nc, CHUNK) + a.shape[3:])
        return jnp.moveaxis(a, 2, 0)

    causal = jnp.tril(jnp.ones((CHUNK, CHUNK), dtype=bool))

    def step(carry, inp):
        C, n, m = carry
        qj, kj, vj, ij, fj = inp
        b = jnp.cumsum(fj, axis=-1)
        g = b[..., -1]
        dmat = b[..., :, None] - b[..., None, :] + ij[..., None, :]
        dmat = jnp.where(causal, dmat, -jnp.inf)
        a = b + m[..., None]
        m_row = jnp.maximum(a, jnp.max(dmat, axis=-1))
        wts = jnp.exp(dmat - m_row[..., None])
        inter = jnp.exp(a - m_row)
        s = jnp.einsum("bhld,bhsd->bhls", qj, kj) * wts
        num = jnp.einsum("bhls,bhsd->bhld", s, vj) + inter[..., None] * jnp.einsum("bhld,bhde->bhle", qj, C)
        den = jnp.sum(s, axis=-1) + inter * jnp.einsum("bhld,bhd->bhl", qj, n)
        h = num / jnp.maximum(jnp.abs(den), jnp.exp(-m_row))[..., None]
        w_s = g[..., None] - b + ij
        m_new = jnp.maximum(g + m, jnp.max(w_s, axis=-1))
        decay = jnp.exp(g + m - m_new)
        ws = jnp.exp(w_s - m_new[..., None])
        C_new = decay[..., None, None] * C + jnp.einsum("bhs,bhsd,bhse->bhde", ws, kj, vj)
        n_new = decay[..., None] * n + jnp.einsum("bhs,bhsd->bhd", ws, kj)
        return (C_new, n_new, m_new), h

    (C, n, m), hs = lax.scan(step, (C0, n0, m0), (chunks(q), chunks(k), chunks(v), chunks(ig), chunks(lf)))
    h = jnp.moveaxis(hs, 0, 2).reshape(Bn, H, T, DH)
    return h, C, n, m


def _mlstm_bidir(q, k, v, ig, lf, C0, n0, m0):
    h_f, Cf, nf, mf = _mlstm_scan(q, k, v, ig[:, 0], lf[:, 0], C0[:, 0], n0[:, 0], m0[:, 0])
    h_b, Cb, nb, mb = _mlstm_scan(jnp.flip(q, 2), jnp.flip(k, 2), jnp.flip(v, 2),
                                  jnp.flip(ig[:, 1], -1), jnp.flip(lf[:, 1], -1), C0[:, 1], n0[:, 1], m0[:, 1])
    h = h_f + jnp.flip(h_b, 2)
    return h, jnp.stack([Cf, Cb], 1), jnp.stack([nf, nb], 1), jnp.stack([mf, mb], 1)


def _centred_mean_minus_self(xg, w):
    T = xg.shape[1]
    S = jnp.concatenate([jnp.zeros_like(xg[:, :1]), jnp.cumsum(xg, axis=1)], axis=1)
    t = jnp.arange(T)
    lo = jnp.maximum(t - w // 2, 0)
    hi = jnp.minimum(t + w // 2, T)
    cnt = (hi - lo).astype(xg.dtype)
    mean = (S[:, hi] - S[:, lo]) / cnt[None, :, None]
    return mean - xg


def _token_mixer(u, C0, n0, m0, w_in, b_if, ln_a_g, w_pa, w_pb, w_pc, w_pool, pool_scale, w_out):
    Bn, T, _ = u.shape
    z = u @ w_in
    zq, zk, zv, zo, zif, zf, zp, zg = jnp.split(z, IN_SPLITS, axis=-1)

    def heads(a):
        return a.astype(F32).reshape(Bn, T, A_HEADS, A_DH).transpose(0, 2, 1, 3)
    q = heads(zq)
    k = heads(zk) * (A_DH ** -0.5)
    v = heads(zv)
    pre = zif.astype(F32).reshape(Bn, T, N_DIR, 2, A_HEADS) + b_if.astype(F32)
    pre = pre.transpose(0, 2, 3, 4, 1)
    ig = pre[:, :, 0]
    lf = jax.nn.log_sigmoid(pre[:, :, 1])
    h, C, n, m = _mlstm_bidir(q, k, v, ig, lf, C0.astype(F32), n0.astype(F32), m0.astype(F32))
    h = h.transpose(0, 2, 1, 3)
    mu = jnp.mean(h, axis=-1, keepdims=True)
    var = jnp.mean(jnp.square(h - mu), axis=-1, keepdims=True)
    hn = ((h - mu) * lax.rsqrt(var + LN_EPS)).reshape(Bn, T, A_W) * ln_a_g.astype(F32)
    y_a = (hn * jax.nn.sigmoid(zo.astype(F32))).astype(u.dtype) @ w_pa

    zf4 = zf.astype(F32).reshape(Bn, T, B_GROUPS, B_GW)
    y_f = jnp.fft.fftn(zf4, axes=(1, 3), norm="ortho").real.reshape(Bn, T, B_W).astype(u.dtype) @ w_pb

    zp4 = zp.astype(F32).reshape(Bn, T, C_GROUPS, C_GW)
    pooled = jnp.stack([_centred_mean_minus_self(zp4[:, :, gi], w) for gi, w in enumerate(POOL_WINDOWS)], axis=2)
    pooled = jnp.einsum("btgc,gce->btge", pooled, w_pool.astype(F32)).reshape(Bn, T, C_W) * pool_scale.astype(F32)
    y_p = pooled.astype(u.dtype) @ w_pc

    ga, gb, gc = jnp.split(jax.nn.sigmoid(zg), N_BRANCH, axis=-1)
    out = (ga * y_a + gb * y_f + gc * y_p) @ w_out
    return out, C, n, m


def _hier_moe(u, w_r1, b_r1, w_r2, b_r2, w_up, w_down):
    Bn, T, D = u.shape
    xt = u.reshape(-1, D)
    xf = xt.astype(F32)
    p_group = jax.nn.softmax(xf @ w_r1.astype(F32) + b_r1.astype(F32), axis=-1)
    g_val, g_idx = lax.top_k(p_group, 1)
    fine = jnp.einsum("nd,dge->nge", xf, w_r2.astype(F32)) + b_r2.astype(F32)
    fine_sel = jnp.take_along_axis(fine, g_idx[:, :, None], axis=1)[:, 0]
    e_val, e_idx = lax.top_k(fine_sel, TOP_K_INNER)
    e_w = jax.nn.softmax(e_val, axis=-1) * g_val
    expert = g_idx * EXP_PER_GROUP + e_idx
    dense_w = jnp.einsum("nk,nke->ne", e_w, jax.nn.one_hot(expert, N_EXPERTS, dtype=F32))
    hid = jnp.einsum("nd,edf->nef", xt, w_up)
    a, b = jnp.split(hid, 2, axis=-1)
    hid = jax.nn.silu(a) * b * dense_w[:, :, None].astype(xt.dtype)
    y = jnp.einsum("nef,efd->nd", hid, w_down)
    return y.reshape(Bn, T, D)


def _layer(x, cond, C0, n0, m0, w_mod, b_mod, w_in, b_if, ln_a_g, w_pa, w_pb, w_pc, w_pool, pool_scale,
           w_out, ln1_g, ln1_b, w_r1, b_r1, w_r2, b_r2, w_up, w_down, ln2_g, ln2_b):
    mod = (jax.nn.silu(cond) @ w_mod + b_mod).reshape(cond.shape[0], 1, 6, D_MODEL)
    shift1, scale1, gate1 = mod[:, :, 0], mod[:, :, 1], mod[:, :, 2]
    shift2, scale2, gate2 = mod[:, :, 3], mod[:, :, 4], mod[:, :, 5]
    u = x * (1.0 + scale1) + shift1
    out, C, n, m = _token_mixer(u, C0, n0, m0, w_in, b_if, ln_a_g, w_pa, w_pb, w_pc, w_pool, pool_scale, w_out)
    x = _layer_norm(DEEPNORM_ALPHA * x + gate1 * out, ln1_g, ln1_b)
    u2 = x * (1.0 + scale2) + shift2
    x = _layer_norm(DEEPNORM_ALPHA * x + gate2 * _hier_moe(u2, w_r1, b_r1, w_r2, b_r2, w_up, w_down), ln2_g, ln2_b)
    return x, C, n, m


def setup_inputs(seed: int = 0) -> dict:
    key = jax.random.key(seed)
    ks = jax.random.split(key, 32)

    def nrm(k, shape, scale):
        return jax.random.normal(k, shape, F32) * scale

    D = D_MODEL
    b_i = nrm(ks[10], (DEPTH, N_DIR, 1, A_HEADS), 0.1)
    b_f = jnp.linspace(3.0, 6.0, A_HEADS, dtype=F32) + nrm(ks[11], (DEPTH, N_DIR, 1, A_HEADS), 0.1)
    return {
        "x_prompt": nrm(ks[0], (BATCH, SEQ, D), 1.0),
        "x_sample": nrm(ks[1], (DEC_BATCH, DEC_SEQ, D), 1.0),
        "c": nrm(ks[2], (DEC_BATCH, D), 1.0),
        "state_C": nrm(ks[3], (DEC_BATCH, DEPTH, N_DIR, A_HEADS, A_DH, A_DH), 0.1),
        "state_n": nrm(ks[4], (DEC_BATCH, DEPTH, N_DIR, A_HEADS, A_DH), 0.5),
        "state_m": nrm(ks[5], (DEC_BATCH, DEPTH, N_DIR, A_HEADS), 1.0) + 2.0,
        "c_ctx": nrm(ks[6], (D,), 1.0),
        "ln_in_g": 1.0 + nrm(ks[7], (D,), 0.02),
        "ln_in_b": nrm(ks[8], (D,), 0.02),
        "w_mod": nrm(ks[9], (DEPTH, D, 6 * D), D ** -0.5),
        "b_mod": nrm(ks[12], (DEPTH, 6 * D), 0.02),
        "w_in": nrm(ks[13], (DEPTH, D, W_IN_COLS), D ** -0.5),
        "b_if": jnp.concatenate([b_i, b_f], axis=2),
        "ln_a_g": 1.0 + nrm(ks[14], (DEPTH, A_W), 0.02),
        "w_pa": nrm(ks[15], (DEPTH, A_W, D), A_W ** -0.5),
        "w_pb": nrm(ks[16], (DEPTH, B_W, D), B_W ** -0.5),
        "w_pc": nrm(ks[17], (DEPTH, C_W, D), C_W ** -0.5),
        "w_pool": nrm(ks[18], (DEPTH, C_GROUPS, C_GW, C_GW), C_GW ** -0.5),
        "pool_scale": 1.0 + nrm(ks[19], (DEPTH, C_W), 0.1),
        "w_out": nrm(ks[20], (DEPTH, D, D), D ** -0.5 * DEEPNORM_BETA),
        "ln1_g": 1.0 + nrm(ks[21], (DEPTH, D), 0.02),
        "ln1_b": nrm(ks[22], (DEPTH, D), 0.02),
        "w_r1": nrm(ks[23], (DEPTH, D, N_GROUPS), D ** -0.5),
        "b_r1": nrm(ks[24], (DEPTH, N_GROUPS), 0.01),
        "w_r2": nrm(ks[25], (DEPTH, D, N_GROUPS, EXP_PER_GROUP), D ** -0.5),
        "b_r2": nrm(ks[26], (DEPTH, N_GROUPS, EXP_PER_GROUP), 0.01),
        "w_up": nrm(ks[27], (DEPTH, N_EXPERTS, D, 2 * D_EXPERT), D ** -0.5),
        "w_down": nrm(ks[28], (DEPTH, N_EXPERTS, D_EXPERT, D), D_EXPERT ** -0.5 * DEEPNORM_BETA),
        "ln2_g": 1.0 + nrm(ks[29], (DEPTH, D), 0.02),
        "ln2_b": nrm(ks[30], (DEPTH, D), 0.02),
    }


def reference(x_prompt, x_sample, c, state_C, state_n, state_m, c_ctx, ln_in_g, ln_in_b, w_mod, b_mod, w_in, b_if,
              ln_a_g, w_pa, w_pb, w_pc, w_pool, pool_scale, w_out, ln1_g, ln1_b, w_r1, b_r1, w_r2, b_r2,
              w_up, w_down, ln2_g, ln2_b):
    layer_params = (w_mod, b_mod, w_in, b_if, ln_a_g, w_pa, w_pb, w_pc, w_pool, pool_scale, w_out,
                    ln1_g, ln1_b, w_r1, b_r1, w_r2, b_r2, w_up, w_down, ln2_g, ln2_b)

    Bp = x_prompt.shape[0]
    xc = _layer_norm(x_prompt, ln_in_g, ln_in_b)
    C0 = jnp.zeros((Bp, N_DIR, A_HEADS, A_DH, A_DH), F32)
    n0 = jnp.zeros((Bp, N_DIR, A_HEADS, A_DH), F32)
    m0 = jnp.full((Bp, N_DIR, A_HEADS), -jnp.inf, F32)
    cond_ctx = c_ctx[None, :]
    Cs, ns, ms = [], [], []
    for l in range(DEPTH):
        lp = [p[l] for p in layer_params]
        xc, C_l, n_l, m_l = _layer(xc, cond_ctx, C0, n0, m0, *lp)
        Cs.append(C_l)
        ns.append(n_l)
        ms.append(m_l)
    new_state_C = jnp.stack(Cs, axis=1)
    new_state_n = jnp.stack(ns, axis=1)
    new_state_m = jnp.stack(ms, axis=1)

    xs = x_sample + _grid_pos_embed(x_sample.shape[1], x_sample.dtype)[None]
    xs = _layer_norm(xs, ln_in_g, ln_in_b)
    for l in range(DEPTH):
        lp = [p[l] for p in layer_params]
        xs, _, _, _ = _layer(xs, c, state_C[:, l], state_n[:, l], state_m[:, l], *lp)

    return (xc, xs, new_state_C, new_state_n, new_state_m)
```

```python
import functools

import numpy as np
import jax
import jax.numpy as jnp
from jax import lax
from jax.experimental import pallas as pl
from jax.experimental.pallas import tpu as pltpu

F32 = jnp.float32
BF16 = jnp.bfloat16

D = 1024
DEPTH = 4
HEADS = 4
DH = 128
A_W = HEADS * DH
B_W = 256
B_GW = 64
C_W = 256
C_GW = 64
POOL_WINDOWS = (2, 4, 8, 16)
POOL_HALO = 8
N_GROUPS = 4
EXP_PER_GROUP = 4
N_EXPERTS = 16
D_EXPERT = 512
GRID_W = 64
POS_BASE = 10000.0
LN_EPS = 1e-5
ALPHA = (2 * DEPTH) ** 0.25
K_SCALE = DH ** -0.5
N_MOD_ROWS = 8

CH = 256
TM = 256
TE = 256
VMEM_LIMIT = 56 * 1024 * 1024


def _dot(a, b):
    return jnp.dot(a, b, preferred_element_type=F32)


def _dot_nt(a, b):
    return lax.dot_general(a, b, (((1,), (1,)), ((), ())), preferred_element_type=F32)


def _split2(x):
    hi = x.astype(BF16)
    lo = (x - hi.astype(F32)).astype(BF16)
    return hi, lo


def _split3(x):
    hi = x.astype(BF16)
    r = x - hi.astype(F32)
    mid = r.astype(BF16)
    lo = (r - mid.astype(F32)).astype(BF16)
    return hi, mid, lo


def _layer_norm(x, g, b):
    mu = jnp.mean(x, axis=-1, keepdims=True)
    xc = x - mu
    var = jnp.mean(xc * xc, axis=-1, keepdims=True)
    return xc * lax.rsqrt(var + LN_EPS) * g + b


def _sigmoid(x):
    return 1.0 / (1.0 + jnp.exp(-x))


def _log_sigmoid(x):
    return jnp.minimum(x, 0.0) - jnp.log1p(jnp.exp(-jnp.abs(x)))


def _params(sem):
    return pltpu.CompilerParams(dimension_semantics=sem, vmem_limit_bytes=VMEM_LIMIT)


def _ln_in_kernel(xp_ref, xs_ref, pos_ref, g_ref, b_ref, o_ref, *, n_ctx_tiles):
    i = pl.program_id(0)

    @pl.when(i < n_ctx_tiles)
    def _():
        o_ref[...] = _layer_norm(xp_ref[...], g_ref[...], b_ref[...])

    @pl.when(i >= n_ctx_tiles)
    def _():
        o_ref[...] = _layer_norm(xs_ref[...] + pos_ref[...], g_ref[...], b_ref[...])


def _ln_in(xp, xs, pos, g, b, t_lat):
    nc, ns = xp.shape[0], xs.shape[0]
    nct, nst, tps = nc // TM, ns // TM, t_lat // TM
    return pl.pallas_call(
        functools.partial(_ln_in_kernel, n_ctx_tiles=nct),
        grid=(nct + nst,),
        in_specs=[
            pl.BlockSpec((TM, D), lambda i: (jnp.minimum(i, nct - 1), 0)),
            pl.BlockSpec((TM, D), lambda i: (jnp.maximum(i - nct, 0), 0)),
            pl.BlockSpec((TM, D), lambda i: (jnp.maximum(i - nct, 0) % tps, 0)),
            pl.BlockSpec((1, D), lambda i: (0, 0)),
            pl.BlockSpec((1, D), lambda i: (0, 0)),
        ],
        out_specs=pl.BlockSpec((TM, D), lambda i: (i, 0)),
        out_shape=jax.ShapeDtypeStruct((nc + ns, D), F32),
        compiler_params=_params(("arbitrary",)),
        name="ln_in",
    )(xp, xs, pos, g, b)


def _mod_kernel(c_ref, w_ref, b_ref, o_ref):
    c = c_ref[...]
    s = c * _sigmoid(c)
    sh, sl = _split2(s)
    wh, wl = _split2(w_ref[...])
    o_ref[...] = _dot(sh, wh) + _dot(sl, wh) + _dot(sh, wl) + b_ref[...]


def _mod_all(cond, w_mod, b_mod):
    nj = 6 * D // 1024
    return pl.pallas_call(
        _mod_kernel,
        grid=(DEPTH, nj),
        in_specs=[
            pl.BlockSpec((N_MOD_ROWS, D), lambda l, j: (0, 0)),
            pl.BlockSpec((None, D, 1024), lambda l, j: (l, 0, j)),
            pl.BlockSpec((None, 1, 1024), lambda l, j: (l, 0, j)),
        ],
        out_specs=pl.BlockSpec((None, N_MOD_ROWS, 1024), lambda l, j: (l, 0, j)),
        out_shape=jax.ShapeDtypeStruct((DEPTH, N_MOD_ROWS, 6 * D), F32),
        compiler_params=_params(("arbitrary", "arbitrary")),
        name="mod",
    )(cond, w_mod, b_mod.reshape(DEPTH, 1, 6 * D))


def _inproj_kernel(x_ref, mod_ref, wqkv_ref, wkt_ref, wfp_ref, wifh_ref, wifl_ref, bif_ref,
                   qkv_ref, kt_ref, fp_ref, gi_ref, gf_ref):
    u = x_ref[...] * (1.0 + mod_ref[1:2, :]) + mod_ref[0:1, :]
    uh, ul = _split2(u)
    z = _dot(uh, wqkv_ref[...])
    qkv_ref[:, 0:A_W] = z[:, 0:A_W].astype(BF16)
    qkv_ref[:, A_W:2 * A_W] = (z[:, A_W:2 * A_W] * K_SCALE).astype(BF16)
    qkv_ref[:, 2 * A_W:3 * A_W] = z[:, 2 * A_W:3 * A_W].astype(BF16)
    kt_ref[...] = (_dot_nt(wkt_ref[...], uh) * K_SCALE).astype(BF16)
    fp_ref[...] = _dot(uh, wfp_ref[...])
    g = _dot(uh, wifh_ref[...]) + _dot(ul, wifh_ref[...]) + _dot(uh, wifl_ref[...]) + bif_ref[...]
    gi_ref[...] = g[:, 0:128]
    gf_ref[...] = g[:, 128:256]


def _inproj(x, mod_l, row_of_tile, wqkv, wkt, wfp, wifh, wifl, bif):
    n = x.shape[0]
    const = lambda i: (0, 0)
    return pl.pallas_call(
        _inproj_kernel,
        grid=(n // TM,),
        in_specs=[
            pl.BlockSpec((TM, D), lambda i: (i, 0)),
            pl.BlockSpec((None, 6, D), lambda i: (row_of_tile(i), 0, 0)),
            pl.BlockSpec((D, 3 * A_W), const),
            pl.BlockSpec((A_W, D), const),
            pl.BlockSpec((D, B_W + C_W), const),
            pl.BlockSpec((D, 256), const),
            pl.BlockSpec((D, 256), const),
            pl.BlockSpec((1, 256), const),
        ],
        out_specs=[
            pl.BlockSpec((TM, 3 * A_W), lambda i: (i, 0)),
            pl.BlockSpec((A_W, TM), lambda i: (0, i)),
            pl.BlockSpec((TM, B_W + C_W), lambda i: (i, 0)),
            pl.BlockSpec((TM, 128), lambda i: (i, 0)),
            pl.BlockSpec((TM, 128), lambda i: (i, 0)),
        ],
        out_shape=[
            jax.ShapeDtypeStruct((n, 3 * A_W), BF16),
            jax.ShapeDtypeStruct((A_W, n), BF16),
            jax.ShapeDtypeStruct((n, B_W + C_W), F32),
            jax.ShapeDtypeStruct((n, 128), F32),
            jax.ShapeDtypeStruct((n, 128), F32),
        ],
        compiler_params=_params(("arbitrary",)),
        name="inproj",
    )(x, mod_l, wqkv, wkt, wfp, wifh, wifl, bif)


def _gates_kernel(gi_ref, gf_ref, git_ref, gft_ref, tri_ref, trit_ref,
                  bcol_ref, rcol_ref, rrow_ref, gsum_ref, rmx_ref):
    tri = tri_ref[...]
    trit = trit_ref[...]
    h, m, l = _split3(_log_sigmoid(gf_ref[...]))
    cum_f = _dot(tri, h) + _dot(tri, m) + _dot(tri, l)
    cum_b = _dot(trit, h) + _dot(trit, m) + _dot(trit, l)
    lane = lax.broadcasted_iota(jnp.int32, (CH, 128), 1)
    bcol = jnp.where(lane < HEADS, cum_f, cum_b)
    bcol_ref[...] = bcol
    rcol_ref[...] = gi_ref[...] - bcol

    h, m, l = _split3(_log_sigmoid(gft_ref[...]))
    row_f = _dot(h, trit) + _dot(m, trit) + _dot(l, trit)
    row_b = _dot(h, tri) + _dot(m, tri) + _dot(l, tri)
    row = lax.broadcasted_iota(jnp.int32, (8, CH), 0)
    brow = jnp.where(row < HEADS, row_f, row_b)
    rrow = git_ref[...] - brow
    rrow_ref[...] = rrow
    row1 = lax.broadcasted_iota(jnp.int32, (8, 1), 0)
    total = jnp.where(row1 < HEADS, brow[:, CH - 1:CH], brow[:, 0:1])
    gsum_ref[...] = jnp.broadcast_to(total, (8, 128))
    rmx_ref[...] = jnp.broadcast_to(jnp.max(rrow, axis=1, keepdims=True), (8, 128))


def _gates(gi, gf, git, gft, tri, trit):
    n = gi.shape[0]
    nch = n // CH
    const = lambda i: (0, 0)
    return pl.pallas_call(
        _gates_kernel,
        grid=(nch,),
        in_specs=[
            pl.BlockSpec((CH, 128), lambda i: (i, 0)),
            pl.BlockSpec((CH, 128), lambda i: (i, 0)),
            pl.BlockSpec((8, CH), lambda i: (0, i)),
            pl.BlockSpec((8, CH), lambda i: (0, i)),
            pl.BlockSpec((CH, CH), const),
            pl.BlockSpec((CH, CH), const),
        ],
        out_specs=[
            pl.BlockSpec((CH, 128), lambda i: (i, 0)),
            pl.BlockSpec((CH, 128), lambda i: (i, 0)),
            pl.BlockSpec((8, CH), lambda i: (0, i)),
            pl.BlockSpec((None, 8, 128), lambda i: (i, 0, 0)),
            pl.BlockSpec((None, 8, 128), lambda i: (i, 0, 0)),
        ],
        out_shape=[
            jax.ShapeDtypeStruct((n, 128), F32),
            jax.ShapeDtypeStruct((n, 128), F32),
            jax.ShapeDtypeStruct((8, n), F32),
            jax.ShapeDtypeStruct((nch, 8, 128), F32),
            jax.ShapeDtypeStruct((nch, 8, 128), F32),
        ],
        compiler_params=_params(("arbitrary",)),
        name="gates",
    )(gi, gf, git, gft, tri, trit)


def _mlstm_kernel(*refs, nchunk, has_state):
    if has_state:
        (q_ref, k_ref, v_ref, kt_ref, bcol_ref, rcol_ref, rrow_ref, gsum_ref, rmx_ref, lng_ref,
         cn0_ref, m0_ref, hn_ref, cn_st) = refs
    else:
        (q_ref, k_ref, v_ref, kt_ref, bcol_ref, rcol_ref, rrow_ref, gsum_ref, rmx_ref, lng_ref,
         hn_ref, cn_out_ref, m_out_ref) = refs
    head = pl.program_id(1)
    lane = lax.broadcasted_iota(jnp.int32, (CH, 128), 1)
    tpos = lax.broadcasted_iota(jnp.int32, (CH, CH), 0)
    spos = lax.broadcasted_iota(jnp.int32, (CH, CH), 1)
    masks = (spos <= tpos, spos >= tpos)

    def col(ref, j, idx):
        blk = ref[j * CH:(j + 1) * CH, :]
        return jnp.sum(jnp.where(lane == idx, blk, 0.0), axis=1, keepdims=True)

    m_in = [[None] * nchunk, [None] * nchunk]
    for d in range(2):
        idx = d * HEADS + head
        cn = cn0_ref[d] if has_state else None
        m = m0_ref[pl.ds(idx, 1), 0:1] if has_state else None
        order = range(nchunk) if d == 0 else range(nchunk - 1, -1, -1)
        for pos_in_sweep, j in enumerate(order):
            if m is not None:
                cn_st[d, j] = cn
                m_in[d][j] = m
            if has_state and pos_in_sweep == nchunk - 1:
                break
            total = gsum_ref[j, pl.ds(idx, 1), 0:1]
            rmax = rmx_ref[j, pl.ds(idx, 1), 0:1]
            mm = rmax if m is None else jnp.maximum(m, rmax)
            ws = jnp.exp(col(rcol_ref, j, idx) - mm)
            vj = v_ref[j * CH:(j + 1) * CH, :].astype(F32)
            vs = jnp.concatenate([(vj * ws).astype(BF16),
                                  jnp.where(lane == 0, ws, 0.0).astype(BF16)], axis=1)
            upd = _dot(kt_ref[:, j * CH:(j + 1) * CH], vs)
            cn = upd if m is None else jnp.exp(m - mm) * cn + upd
            m = total + mm
        if not has_state:
            cn_out_ref[d] = cn
            m_out_ref[d:d + 1, :] = jnp.broadcast_to(m, (1, 128))

    for j in range(nchunk):
        sl = slice(j * CH, (j + 1) * CH)
        qj = q_ref[sl, :]
        s_raw = _dot_nt(qj, k_ref[sl, :])
        p_sum, h_inter = None, None
        for d in range(2):
            idx = d * HEADS + head
            rr = rrow_ref[pl.ds(idx, 1), sl]
            m_row = jnp.max(jnp.where(masks[d], rr, -jnp.inf), axis=1, keepdims=True)
            m_prev = m_in[d][j]
            if m_prev is not None:
                m_row = jnp.maximum(m_row, m_prev)
            p = s_raw * jnp.where(masks[d], jnp.exp(rr - m_row), 0.0)
            den = jnp.sum(p, axis=1, keepdims=True)
            if m_prev is not None:
                inter = jnp.exp(m_prev - m_row)
                qc = _dot(qj, cn_st[d, j].astype(BF16))
                den = den + inter * qc[:, DH:DH + 1]
            floor = jnp.exp(-(col(bcol_ref, j, idx) + m_row))
            rinv = 1.0 / jnp.maximum(jnp.abs(den), floor)
            p_sum = p * rinv if p_sum is None else p_sum + p * rinv
            if m_prev is not None:
                hi = (inter * rinv) * qc[:, 0:DH]
                h_inter = hi if h_inter is None else h_inter + hi
        hj = _dot(p_sum.astype(BF16), v_ref[sl, :])
        if h_inter is not None:
            hj = hj + h_inter
        mu = jnp.mean(hj, axis=-1, keepdims=True)
        hc = hj - mu
        var = jnp.mean(hc * hc, axis=-1, keepdims=True)
        hn_ref[sl, :] = hc * lax.rsqrt(var + LN_EPS) * lng_ref[...]


def _mlstm(qkv, kt, bcol, rcol, rrow, gsum, rmx, lng, tok0, nseq, t, cn0=None, m0=None):
    nchunk = t // CH
    sb = tok0 // t
    cb = tok0 // CH
    has_state = cn0 is not None
    in_specs = [
        pl.BlockSpec((t, DH), lambda s, h: (sb + s, h)),
        pl.BlockSpec((t, DH), lambda s, h: (sb + s, HEADS + h)),
        pl.BlockSpec((t, DH), lambda s, h: (sb + s, 2 * HEADS + h)),
        pl.BlockSpec((DH, t), lambda s, h: (h, sb + s)),
        pl.BlockSpec((t, 128), lambda s, h: (sb + s, 0)),
        pl.BlockSpec((t, 128), lambda s, h: (sb + s, 0)),
        pl.BlockSpec((8, t), lambda s, h: (0, sb + s)),
        pl.BlockSpec((nchunk, 8, 128), lambda s, h: (cb // nchunk + s, 0, 0)),
        pl.BlockSpec((nchunk, 8, 128), lambda s, h: (cb // nchunk + s, 0, 0)),
        pl.BlockSpec((1, DH), lambda s, h: (0, h)),
    ]
    args = [qkv, qkv, qkv, kt, bcol, rcol, rrow, gsum, rmx, lng]
    hn_spec = pl.BlockSpec((t, DH), lambda s, h: (s, h))
    hn_shape = jax.ShapeDtypeStruct((nseq * t, A_W), F32)
    if has_state:
        in_specs += [
            pl.BlockSpec((None, 2, None, DH, 2 * DH), lambda s, h: (s, 0, h, 0, 0)),
            pl.BlockSpec((None, 8, 128), lambda s, h: (s, 0, 0)),
        ]
        args += [cn0, m0]
        out_specs, out_shape = hn_spec, hn_shape
        scratch = [pltpu.VMEM((2, nchunk, DH, 2 * DH), F32)]
    else:
        out_specs = [
            hn_spec,
            pl.BlockSpec((None, 2, None, DH, 2 * DH), lambda s, h: (s, 0, h, 0, 0)),
            pl.BlockSpec((None, None, 2, 128), lambda s, h: (s, h, 0, 0)),
        ]
        out_shape = [
            hn_shape,
            jax.ShapeDtypeStruct((nseq, 2, HEADS, DH, 2 * DH), F32),
            jax.ShapeDtypeStruct((nseq, HEADS, 2, 128), F32),
        ]
        scratch = []
    return pl.pallas_call(
        functools.partial(_mlstm_kernel, nchunk=nchunk, has_state=has_state),
        grid=(nseq, HEADS),
        in_specs=in_specs,
        out_specs=out_specs,
        out_shape=out_shape,
        scratch_shapes=scratch,
        compiler_params=_params(("arbitrary", "arbitrary")),
        name="mlstm_lat" if has_state else "mlstm_ctx",
    )(*args)


def _fftpool_kernel(fp_ref, cth_ref, ctl_ref, sth_ref, stl_ref, bcsh_ref, bcsl_ref, icnt_ref,
                    wpool_ref, pscale_ref, yf_ref, pp_ref, pad_ref, *, t):
    fh, fl = _split2(fp_ref[:, 0:B_W])
    xcs = _dot(fh, bcsh_ref[...]) + _dot(fl, bcsh_ref[...]) + _dot(fh, bcsl_ref[...])
    xch, xcl = _split2(xcs[:, 0:B_W])
    xsh, xsl = _split2(xcs[:, B_W:2 * B_W])
    y = (_dot(cth_ref[...], xch) + _dot(ctl_ref[...], xch) + _dot(cth_ref[...], xcl)
         - _dot(sth_ref[...], xsh) - _dot(stl_ref[...], xsh) - _dot(sth_ref[...], xsl))
    yf_ref[...] = (y * ((t * B_GW) ** -0.5)).astype(BF16)

    p = fp_ref[:, B_W:B_W + C_W]
    zero = jnp.zeros((POOL_HALO, C_W), F32)
    pad_ref[0:POOL_HALO, :] = zero
    pad_ref[POOL_HALO + t:2 * POOL_HALO + t, :] = zero
    pad_ref[POOL_HALO:POOL_HALO + t, :] = p

    def sh(dlt):
        return pad_ref[POOL_HALO + dlt:POOL_HALO + dlt + t, :]

    w2 = sh(-1) + p
    w4 = w2 + sh(-2) + sh(1)
    w8 = w4 + sh(-4) + sh(-3) + sh(2) + sh(3)
    w16 = w8 + sh(-8) + sh(-7) + sh(-6) + sh(-5) + sh(4) + sh(5) + sh(6) + sh(7)
    grp = lax.broadcasted_iota(jnp.int32, (t, C_W), 1) // C_GW
    wsum = jnp.where(grp == 0, w2, jnp.where(grp == 1, w4, jnp.where(grp == 2, w8, w16)))
    pooled = wsum * icnt_ref[...] - p
    pp_ref[...] = (_dot(pooled.astype(BF16), wpool_ref[...]) * pscale_ref[...]).astype(BF16)


def _fftpool(fp, consts, wpool_bd, pscale, tok0, nseq, t):
    cth, ctl, sth, stl, bcsh, bcsl, icnt = consts
    sb = tok0 // t
    const = lambda s: (0, 0)
    return pl.pallas_call(
        functools.partial(_fftpool_kernel, t=t),
        grid=(nseq,),
        in_specs=[
            pl.BlockSpec((t, B_W + C_W), lambda s: (sb + s, 0)),
            pl.BlockSpec((t, t), const), pl.BlockSpec((t, t), const),
            pl.BlockSpec((t, t), const), pl.BlockSpec((t, t), const),
            pl.BlockSpec((B_W, 2 * B_W), const), pl.BlockSpec((B_W, 2 * B_W), const),
            pl.BlockSpec((t, C_W), const),
            pl.BlockSpec((C_W, C_W), const),
            pl.BlockSpec((1, C_W), const),
        ],
        out_specs=[pl.BlockSpec((t, B_W), lambda s: (s, 0)), pl.BlockSpec((t, C_W), lambda s: (s, 0))],
        out_shape=[jax.ShapeDtypeStruct((nseq * t, B_W), BF16), jax.ShapeDtypeStruct((nseq * t, C_W), BF16)],
        scratch_shapes=[pltpu.VMEM((t + 2 * POOL_HALO, C_W), F32)],
        compiler_params=_params(("arbitrary",)),
        name=f"fftpool_{t}",
    )(fp, cth, ctl, sth, stl, bcsh, bcsl, icnt, wpool_bd, pscale)


def _merge_kernel(x_ref, mod_ref, hn_ref, yf_ref, pp_ref, wo_ref, wg_ref, wpa_ref, wpb_ref, wpc_ref,
                  wout_ref, g1_ref, b1_ref, wrh_ref, wrl_ref, br_ref, x1_ref, u2_ref, route_ref):
    x = x_ref[...]
    ub = (x * (1.0 + mod_ref[1:2, :]) + mod_ref[0:1, :]).astype(BF16)
    a = (hn_ref[...] * _sigmoid(_dot(ub, wo_ref[...]))).astype(BF16)
    acc = _sigmoid(_dot(ub, wg_ref[:, 0:D])) * _dot(a, wpa_ref[...])
    acc = acc + _sigmoid(_dot(ub, wg_ref[:, D:2 * D])) * _dot(yf_ref[...], wpb_ref[...])
    acc = acc + _sigmoid(_dot(ub, wg_ref[:, 2 * D:3 * D])) * _dot(pp_ref[...], wpc_ref[...])
    out = _dot(acc.astype(BF16), wout_ref[...])
    x1 = _layer_norm(ALPHA * x + mod_ref[2:3, :] * out, g1_ref[...], b1_ref[...])
    x1_ref[...] = x1
    u2 = x1 * (1.0 + mod_ref[4:5, :]) + mod_ref[3:4, :]
    u2_ref[...] = u2

    uh, ul = _split2(u2)
    lt = _dot_nt(wrh_ref[...], uh) + _dot_nt(wrh_ref[...], ul) + _dot_nt(wrl_ref[...], uh) + br_ref[:, 0:1]
    gl = [lt[g:g + 1, :] for g in range(N_GROUPS)]
    gmax = jnp.maximum(jnp.maximum(gl[0], gl[1]), jnp.maximum(gl[2], gl[3]))
    gidx = jnp.where(gl[0] >= gmax, 0, jnp.where(gl[1] >= gmax, 1, jnp.where(gl[2] >= gmax, 2, 3)))
    gval = 1.0 / (jnp.exp(gl[0] - gmax) + jnp.exp(gl[1] - gmax) + jnp.exp(gl[2] - gmax) + jnp.exp(gl[3] - gmax))
    fe = []
    for e in range(EXP_PER_GROUP):
        r = [lt[N_GROUPS + g * EXP_PER_GROUP + e:N_GROUPS + g * EXP_PER_GROUP + e + 1, :] for g in range(N_GROUPS)]
        fe.append(jnp.where(gidx == 0, r[0], jnp.where(gidx == 1, r[1], jnp.where(gidx == 2, r[2], r[3]))))

    def top1(v):
        vmax = jnp.maximum(jnp.maximum(v[0], v[1]), jnp.maximum(v[2], v[3]))
        imax = jnp.where(v[0] >= vmax, 0, jnp.where(v[1] >= vmax, 1, jnp.where(v[2] >= vmax, 2, 3)))
        return vmax, imax

    v1, i1 = top1(fe)
    v2, i2 = top1([jnp.where(i1 == e, -jnp.inf, fe[e]) for e in range(EXP_PER_GROUP)])
    tt = jnp.exp(v2 - v1)
    w1 = gval / (1.0 + tt)
    w2 = gval * tt / (1.0 + tt)
    e1 = (gidx * EXP_PER_GROUP + i1).astype(F32)
    e2 = (gidx * EXP_PER_GROUP + i2).astype(F32)
    route_ref[...] = jnp.zeros(route_ref.shape, F32)
    route_ref[0:1, :] = e1
    route_ref[1:2, :] = e2
    route_ref[2:3, :] = w1
    route_ref[3:4, :] = w2


def _merge(x, mod_l, row_of_tile, hn, yf, pp, wo, wg, wpa, wpb, wpc, wout, g1, b1, wrh, wrl, br):
    n = x.shape[0]
    const = lambda i: (0, 0)
    tok = lambda i: (i, 0)
    return pl.pallas_call(
        _merge_kernel,
        grid=(n // TM,),
        in_specs=[
            pl.BlockSpec((TM, D), tok),
            pl.BlockSpec((None, 6, D), lambda i: (row_of_tile(i), 0, 0)),
            pl.BlockSpec((TM, A_W), tok),
            pl.BlockSpec((TM, B_W), tok),
            pl.BlockSpec((TM, C_W), tok),
            pl.BlockSpec((D, A_W), const),
            pl.BlockSpec((D, 3 * D), const),
            pl.BlockSpec((A_W, D), const),
            pl.BlockSpec((B_W, D), const),
            pl.BlockSpec((C_W, D), const),
            pl.BlockSpec((D, D), const),
            pl.BlockSpec((1, D), const),
            pl.BlockSpec((1, D), const),
            pl.BlockSpec((32, D), const),
            pl.BlockSpec((32, D), const),
            pl.BlockSpec((32, 128), const),
        ],
        out_specs=[pl.BlockSpec((TM, D), tok), pl.BlockSpec((TM, D), tok), pl.BlockSpec((8, TM), lambda i: (0, i))],
        out_shape=[jax.ShapeDtypeStruct((n, D), F32), jax.ShapeDtypeStruct((n, D), F32),
                   jax.ShapeDtypeStruct((8, n), F32)],
        compiler_params=_params(("arbitrary",)),
        name="merge",
    )(x, mod_l, hn, yf, pp, wo, wg, wpa, wpb, wpc, wout, g1, b1, wrh, wrl, br)


def _row_copy(src_hbm, dst_ref, sem, src_row, dst_row):
    return pltpu.make_async_copy(src_hbm.at[pl.ds(src_row, 1)], dst_ref.at[pl.ds(dst_row, 1)], sem)


def _dispatch_kernel(src_ref, nused_ref, u_hbm, o_ref, sem):
    i = pl.program_id(0)

    @pl.when(i < nused_ref[0])
    def _():
        def issue(r, c):
            _row_copy(u_hbm, o_ref, sem, src_ref[i * TE + r], r).start()
            return c

        def wait(r, c):
            _row_copy(u_hbm, o_ref, sem, 0, r).wait()
            return c

        lax.fori_loop(0, TE, issue, 0)
        lax.fori_loop(0, TE, wait, 0)

    @pl.when(i >= nused_ref[0])
    def _():
        o_ref[...] = jnp.zeros(o_ref.shape, F32)


def _dispatch(src, nused, u2, p_rows):
    return pl.pallas_call(
        _dispatch_kernel,
        grid_spec=pltpu.PrefetchScalarGridSpec(
            num_scalar_prefetch=2,
            grid=(p_rows // TE,),
            in_specs=[pl.BlockSpec(memory_space=pl.ANY)],
            out_specs=pl.BlockSpec((TE, D), lambda i, src, nu: (i, 0)),
            scratch_shapes=[pltpu.SemaphoreType.DMA(())],
        ),
        out_shape=jax.ShapeDtypeStruct((p_rows, D), F32),
        compiler_params=_params(("arbitrary",)),
        name="dispatch",
    )(src, nused, u2)


def _experts_kernel(te_ref, nused_ref, xs_ref, wup_ref, wdn_ref, ys_ref):
    @pl.when(pl.program_id(0) < nused_ref[0])
    def _():
        hid = _dot(xs_ref[...].astype(BF16), wup_ref[...].astype(BF16))
        a = hid[:, 0:D_EXPERT]
        act = (a * _sigmoid(a) * hid[:, D_EXPERT:2 * D_EXPERT]).astype(BF16)
        ys_ref[...] = _dot(act, wdn_ref[...].astype(BF16))

    @pl.when(pl.program_id(0) >= nused_ref[0])
    def _():
        ys_ref[...] = jnp.zeros(ys_ref.shape, F32)


def _experts(tile_expert, nused, xs, w_up_l, w_down_l):
    p_rows = xs.shape[0]
    row = lambda i, te, nu: (jnp.minimum(i, nu[0] - 1), 0)
    return pl.pallas_call(
        _experts_kernel,
        grid_spec=pltpu.PrefetchScalarGridSpec(
            num_scalar_prefetch=2,
            grid=(p_rows // TE,),
            in_specs=[
                pl.BlockSpec((TE, D), row),
                pl.BlockSpec((None, D, 2 * D_EXPERT), lambda i, te, nu: (te[i], 0, 0)),
                pl.BlockSpec((None, D_EXPERT, D), lambda i, te, nu: (te[i], 0, 0)),
            ],
            out_specs=pl.BlockSpec((TE, D), lambda i, te, nu: (i, 0)),
        ),
        out_shape=jax.ShapeDtypeStruct((p_rows, D), F32),
        compiler_params=_params(("arbitrary",)),
        name="experts",
    )(tile_expert, nused, xs, w_up_l, w_down_l)


def _combine_kernel(pos_ref, ys_hbm, x1_ref, mod_ref, rt_ref, g2_ref, b2_ref, o_ref, buf, sem, *, n):
    i = pl.program_id(0)

    def issue(r, c):
        _row_copy(ys_hbm, buf.at[0], sem, pos_ref[i * TM + r], r).start()
        _row_copy(ys_hbm, buf.at[1], sem, pos_ref[n + i * TM + r], r).start()
        return c

    def wait(r, c):
        _row_copy(ys_hbm, buf.at[0], sem, 0, r).wait()
        _row_copy(ys_hbm, buf.at[1], sem, 0, r).wait()
        return c

    lax.fori_loop(0, TM, issue, 0)
    lax.fori_loop(0, TM, wait, 0)
    y = rt_ref[:, 2:3] * buf[0] + rt_ref[:, 3:4] * buf[1]
    o_ref[...] = _layer_norm(ALPHA * x1_ref[...] + mod_ref[5:6, :] * y, g2_ref[...], b2_ref[...])


def _combine(pos, ys, x1, mod_l, row_of_tile, route_t, g2, b2):
    n = x1.shape[0]
    return pl.pallas_call(
        functools.partial(_combine_kernel, n=n),
        grid_spec=pltpu.PrefetchScalarGridSpec(
            num_scalar_prefetch=1,
            grid=(n // TM,),
            in_specs=[
                pl.BlockSpec(memory_space=pl.ANY),
                pl.BlockSpec((TM, D), lambda i, pos: (i, 0)),
                pl.BlockSpec((None, 6, D), lambda i, pos: (row_of_tile(i), 0, 0)),
                pl.BlockSpec((TM, 8), lambda i, pos: (i, 0)),
                pl.BlockSpec((1, D), lambda i, pos: (0, 0)),
                pl.BlockSpec((1, D), lambda i, pos: (0, 0)),
            ],
            out_specs=pl.BlockSpec((TM, D), lambda i, pos: (i, 0)),
            scratch_shapes=[pltpu.VMEM((2, TM, D), F32), pltpu.SemaphoreType.DMA(())],
        ),
        out_shape=jax.ShapeDtypeStruct((n, D), F32),
        compiler_params=_params(("arbitrary",)),
        name="combine",
    )(pos, ys, x1, mod_l, route_t, g2, b2)


def _routing_tables(route, n, p_rows):
    e_flat = route[0:2].astype(jnp.int32).reshape(2 * n)
    onehot = (e_flat[:, None] == jnp.arange(N_EXPERTS, dtype=jnp.int32)[None, :]).astype(jnp.int32)
    csum = jnp.cumsum(onehot, axis=0)
    rank = jnp.sum(onehot * (csum - 1), axis=1)
    cnt = csum[-1]
    cnt_pad = ((cnt + TE - 1) // TE) * TE
    end = jnp.cumsum(cnt_pad)
    off = end - cnt_pad
    pos = jnp.sum(onehot * off[None, :], axis=1) + rank
    tok = jnp.arange(2 * n, dtype=jnp.int32) % n
    src = jnp.zeros((p_rows,), jnp.int32).at[pos].set(tok, unique_indices=True)
    tile_start = jnp.arange(p_rows // TE, dtype=jnp.int32) * TE
    tile_expert = jnp.minimum(jnp.sum((tile_start[:, None] >= end[None, :]).astype(jnp.int32), axis=1),
                              N_EXPERTS - 1)
    nused = (end[-1] // TE).reshape(1).astype(jnp.int32)
    return pos.astype(jnp.int32), src, tile_expert.astype(jnp.int32), nused


def _grid_pos_embed(n_tok):
    rows = n_tok // GRID_W
    r, col = np.meshgrid(np.arange(rows, dtype=np.float32), np.arange(GRID_W, dtype=np.float32), indexing="ij")
    r, col = r.reshape(-1), col.reshape(-1)
    quarter = D // 4
    freq = (1.0 / (np.float32(POS_BASE) ** (np.arange(quarter, dtype=np.float32) / np.float32(quarter)))).astype(np.float32)
    ang_r = r[:, None] * freq[None, :]
    ang_c = col[:, None] * freq[None, :]
    return jnp.asarray(np.concatenate([np.sin(ang_r), np.cos(ang_r), np.sin(ang_c), np.cos(ang_c)], axis=-1),
                       dtype=F32)


def _dft_consts(t):
    def hl(a):
        hi = a.astype(BF16)
        lo = (a - hi.astype(np.float64)).astype(BF16)
        return jnp.asarray(hi), jnp.asarray(lo)

    tt = np.arange(t)
    ang = 2.0 * np.pi * ((tt[:, None] * tt[None, :]) % t) / t
    cth, ctl = hl(np.cos(ang))
    sth, stl = hl(np.sin(ang))
    cc = np.arange(B_GW)
    angc = 2.0 * np.pi * ((cc[:, None] * cc[None, :]) % B_GW) / B_GW
    eye = np.eye(B_W // B_GW)
    bcs = np.concatenate([np.kron(eye, np.cos(angc)), np.kron(eye, np.sin(angc))], axis=1)
    bcsh, bcsl = hl(bcs)
    pos = np.arange(t)
    icnt = np.concatenate(
        [np.repeat((1.0 / (np.minimum(pos + w // 2, t) - np.maximum(pos - w // 2, 0)))[:, None], C_GW, axis=1)
         for w in POOL_WINDOWS], axis=1)
    return cth, ctl, sth, stl, bcsh, bcsl, jnp.asarray(icnt, F32)


def kernel(x_prompt, x_sample, c, state_C, state_n, state_m, c_ctx, ln_in_g, ln_in_b, w_mod, b_mod, w_in, b_if, ln_a_g, w_pa, w_pb, w_pc, w_pool, pool_scale, w_out, ln1_g, ln1_b, w_r1, b_r1, w_r2, b_r2, w_up, w_down, ln2_g, ln2_b):
    bc, tc, _ = x_prompt.shape
    bl, tl, _ = x_sample.shape
    nc, nl = bc * tc, bl * tl
    n = nc + nl
    assert tc == CH and tl % CH == 0 and nc % tl == 0 and bl + 1 <= N_MOD_ROWS
    p_rows = 2 * n + N_EXPERTS * TE

    def row_of_tile(i):
        return jnp.where(i < nc // TM, 0, 1 + jnp.maximum(i - nc // TM, 0) // (tl // TM))

    wqkv = w_in[:, :, 0:3 * A_W].astype(BF16)
    wkt = jnp.swapaxes(w_in[:, :, A_W:2 * A_W], 1, 2).astype(BF16)
    wo = w_in[:, :, 3 * A_W:4 * A_W].astype(BF16)
    c0 = 4 * A_W
    wif = w_in[:, :, c0:c0 + 16].reshape(DEPTH, D, 2, 2, HEADS)
    zpad = jnp.zeros((DEPTH, D, 128 - 2 * HEADS), F32)
    wif = jnp.concatenate([wif[:, :, :, 0, :].reshape(DEPTH, D, 2 * HEADS), zpad,
                           wif[:, :, :, 1, :].reshape(DEPTH, D, 2 * HEADS), zpad], axis=2)
    wifh, wifl = _split2(wif)
    bpad = jnp.zeros((DEPTH, 128 - 2 * HEADS), F32)
    bif = jnp.concatenate([b_if[:, :, 0, :].reshape(DEPTH, 2 * HEADS), bpad,
                           b_if[:, :, 1, :].reshape(DEPTH, 2 * HEADS), bpad], axis=1).reshape(DEPTH, 1, 256)
    wfp = w_in[:, :, c0 + 16:c0 + 16 + B_W + C_W].astype(BF16)
    wg = w_in[:, :, c0 + 16 + B_W + C_W:].astype(BF16)
    wpa, wpb, wpc, wout = (w.astype(BF16) for w in (w_pa, w_pb, w_pc, w_out))
    eye = jnp.eye(C_W // C_GW, dtype=F32)
    wpool_bd = jnp.einsum("gh,lgce->lgche", eye, w_pool).reshape(DEPTH, C_W, C_W).astype(BF16)
    wr = jnp.concatenate([w_r1, w_r2.reshape(DEPTH, D, N_EXPERTS),
                          jnp.zeros((DEPTH, D, 32 - N_GROUPS - N_EXPERTS), F32)], axis=2)
    wrh, wrl = _split2(jnp.swapaxes(wr, 1, 2))
    br = jnp.concatenate([b_r1, b_r2.reshape(DEPTH, N_EXPERTS),
                          jnp.zeros((DEPTH, 32 - N_GROUPS - N_EXPERTS), F32)], axis=1)
    br = jnp.broadcast_to(br[:, :, None], (DEPTH, 32, 128))

    tri_np = np.tril(np.ones((CH, CH), np.float32))
    tri, trit = jnp.asarray(tri_np, BF16), jnp.asarray(tri_np.T, BF16)
    consts_c, consts_l = _dft_consts(tc), _dft_consts(tl)

    cond = jnp.concatenate([c_ctx[None, :], c, jnp.zeros((N_MOD_ROWS - 1 - bl, D), F32)], axis=0)
    mod = _mod_all(cond, w_mod, b_mod).reshape(DEPTH, N_MOD_ROWS, 6, D)

    x = _ln_in(x_prompt.reshape(nc, D), x_sample.reshape(nl, D), _grid_pos_embed(tl),
               ln_in_g.reshape(1, D), ln_in_b.reshape(1, D), tl)

    cn0 = jnp.concatenate([state_C, state_n[..., None], jnp.zeros(state_C.shape[:-1] + (DH - 1,), F32)], axis=-1)
    m0 = jnp.broadcast_to(state_m.reshape(bl, DEPTH, 2 * HEADS, 1), (bl, DEPTH, 2 * HEADS, 128))

    new_c, new_n, new_m = [], [], []
    for l in range(DEPTH):
        mod_l = mod[l]
        qkv, kt, fp, gi, gf = _inproj(x, mod_l, row_of_tile, wqkv[l], wkt[l], wfp[l], wifh[l], wifl[l], bif[l])
        bcol, rcol, rrow, gsum, rmx = _gates(gi, gf, gi[:, 0:8].T, gf[:, 0:8].T, tri, trit)
        lng = ln_a_g[l].reshape(1, A_W)
        hn_c, cn_c, m_c = _mlstm(qkv, kt, bcol, rcol, rrow, gsum, rmx, lng, 0, bc, tc)
        hn_l = _mlstm(qkv, kt, bcol, rcol, rrow, gsum, rmx, lng, nc, bl, tl, cn0[:, l], m0[:, l])
        pscale = pool_scale[l].reshape(1, C_W)
        yf_c, pp_c = _fftpool(fp, consts_c, wpool_bd[l], pscale, 0, bc, tc)
        yf_l, pp_l = _fftpool(fp, consts_l, wpool_bd[l], pscale, nc, bl, tl)
        hn = jnp.concatenate([hn_c, hn_l], axis=0)
        yf = jnp.concatenate([yf_c, yf_l], axis=0)
        pp = jnp.concatenate([pp_c, pp_l], axis=0)
        x1, u2, route = _merge(x, mod_l, row_of_tile, hn, yf, pp, wo[l], wg[l], wpa[l], wpb[l], wpc[l], wout[l],
                               ln1_g[l].reshape(1, D), ln1_b[l].reshape(1, D), wrh[l], wrl[l], br[l])
        pos, src, tile_expert, nused = _routing_tables(route, n, p_rows)
        xs = _dispatch(src, nused, u2, p_rows)
        ys = _experts(tile_expert, nused, xs, w_up[l], w_down[l])
        x = _combine(pos, ys, x1, mod_l, row_of_tile, route.T, ln2_g[l].reshape(1, D), ln2_b[l].reshape(1, D))
        new_c.append(cn_c[..., 0:DH])
        new_n.append(cn_c[..., DH])
        new_m.append(jnp.swapaxes(m_c[..., 0], 1, 2))

    return (x[0:nc].reshape(bc, tc, D), x[nc:].reshape(bl, tl, D),
            jnp.stack(new_c, axis=1), jnp.stack(new_n, axis=1), jnp.stack(new_m, axis=1))
```

```python
import functools

import numpy as np
import jax
import jax.numpy as jnp
from jax import lax
from jax.experimental import pallas as pl
from jax.experimental.pallas import tpu as pltpu

F32 = jnp.float32
BF16 = jnp.bfloat16

D = 1024
DEPTH = 4
HEADS = 4
DH = 128
A_W = HEADS * DH
B_W = 256
B_GW = 64
C_W = 256
C_GW = 64
POOL_WINDOWS = (2, 4, 8, 16)
POOL_HALO = 8
N_GROUPS = 4
EXP_PER_GROUP = 4
N_EXPERTS = 16
D_EXPERT = 512
GRID_W = 64
POS_BASE = 10000.0
LN_EPS = 1e-5
ALPHA = (2 * DEPTH) ** 0.25
K_SCALE = DH ** -0.5
N_MOD_ROWS = 8

CH = 256
TM = 256
TE = 256
W_ROWS = 8
RL = 2 * TM + N_EXPERTS * W_ROWS
NSLOT = RL // W_ROWS
WPT = TE // W_ROWS
EP = 128
VMEM_LIMIT = 56 * 1024 * 1024


def _dot(a, b):
    return jnp.dot(a, b, preferred_element_type=F32)


def _dot_nt(a, b):
    return lax.dot_general(a, b, (((1,), (1,)), ((), ())), preferred_element_type=F32)


def _split2(x):
    hi = x.astype(BF16)
    lo = (x - hi.astype(F32)).astype(BF16)
    return hi, lo


def _split3(x):
    hi = x.astype(BF16)
    r = x - hi.astype(F32)
    mid = r.astype(BF16)
    lo = (r - mid.astype(F32)).astype(BF16)
    return hi, mid, lo


def _layer_norm(x, g, b):
    mu = jnp.mean(x, axis=-1, keepdims=True)
    xc = x - mu
    var = jnp.mean(xc * xc, axis=-1, keepdims=True)
    return xc * lax.rsqrt(var + LN_EPS) * g + b


def _sigmoid(x):
    return 1.0 / (1.0 + jnp.exp(-x))


def _log_sigmoid(x):
    return jnp.minimum(x, 0.0) - jnp.log1p(jnp.exp(-jnp.abs(x)))


def _params(sem):
    return pltpu.CompilerParams(dimension_semantics=sem, vmem_limit_bytes=VMEM_LIMIT)


def _ln_in_kernel(xp_ref, xs_ref, pos_ref, g_ref, b_ref, o_ref, *, n_ctx_tiles):
    i = pl.program_id(0)

    @pl.when(i < n_ctx_tiles)
    def _():
        o_ref[...] = _layer_norm(xp_ref[...], g_ref[...], b_ref[...])

    @pl.when(i >= n_ctx_tiles)
    def _():
        o_ref[...] = _layer_norm(xs_ref[...] + pos_ref[...], g_ref[...], b_ref[...])


def _ln_in(xp, xs, pos, g, b, t_lat):
    nc, ns = xp.shape[0], xs.shape[0]
    nct, nst, tps = nc // TM, ns // TM, t_lat // TM
    return pl.pallas_call(
        functools.partial(_ln_in_kernel, n_ctx_tiles=nct),
        grid=(nct + nst,),
        in_specs=[
            pl.BlockSpec((TM, D), lambda i: (jnp.minimum(i, nct - 1), 0)),
            pl.BlockSpec((TM, D), lambda i: (jnp.maximum(i - nct, 0), 0)),
            pl.BlockSpec((TM, D), lambda i: (jnp.maximum(i - nct, 0) % tps, 0)),
            pl.BlockSpec((1, D), lambda i: (0, 0)),
            pl.BlockSpec((1, D), lambda i: (0, 0)),
        ],
        out_specs=pl.BlockSpec((TM, D), lambda i: (i, 0)),
        out_shape=jax.ShapeDtypeStruct((nc + ns, D), F32),
        compiler_params=_params(("arbitrary",)),
        name="ln_in",
    )(xp, xs, pos, g, b)


def _mod_kernel(c_ref, w_ref, b_ref, o_ref):
    c = c_ref[...]
    s = c * _sigmoid(c)
    sh, sl = _split2(s)
    wh, wl = _split2(w_ref[...])
    o_ref[...] = _dot(sh, wh) + _dot(sl, wh) + _dot(sh, wl) + b_ref[...]


def _mod_all(cond, w_mod, b_mod):
    nj = 6 * D // 1024
    return pl.pallas_call(
        _mod_kernel,
        grid=(DEPTH, nj),
        in_specs=[
            pl.BlockSpec((N_MOD_ROWS, D), lambda l, j: (0, 0)),
            pl.BlockSpec((None, D, 1024), lambda l, j: (l, 0, j)),
            pl.BlockSpec((None, 1, 1024), lambda l, j: (l, 0, j)),
        ],
        out_specs=pl.BlockSpec((None, N_MOD_ROWS, 1024), lambda l, j: (l, 0, j)),
        out_shape=jax.ShapeDtypeStruct((DEPTH, N_MOD_ROWS, 6 * D), F32),
        compiler_params=_params(("arbitrary", "arbitrary")),
        name="mod",
    )(cond, w_mod, b_mod.reshape(DEPTH, 1, 6 * D))


def _inproj_kernel(x_ref, mod_ref, wqkv_ref, wkt_ref, wfp_ref, wifh_ref, wifl_ref, bif_ref,
                   qkv_ref, kt_ref, fp_ref, gi_ref, gf_ref):
    u = x_ref[...] * (1.0 + mod_ref[1:2, :]) + mod_ref[0:1, :]
    uh, ul = _split2(u)
    z = _dot(uh, wqkv_ref[...])
    qkv_ref[:, 0:A_W] = z[:, 0:A_W].astype(BF16)
    qkv_ref[:, A_W:2 * A_W] = (z[:, A_W:2 * A_W] * K_SCALE).astype(BF16)
    qkv_ref[:, 2 * A_W:3 * A_W] = z[:, 2 * A_W:3 * A_W].astype(BF16)
    kt_ref[...] = (_dot_nt(wkt_ref[...], uh) * K_SCALE).astype(BF16)
    fp_ref[...] = _dot(uh, wfp_ref[...])
    g = _dot(uh, wifh_ref[...]) + _dot(ul, wifh_ref[...]) + _dot(uh, wifl_ref[...]) + bif_ref[...]
    gi_ref[...] = g[:, 0:128]
    gf_ref[...] = g[:, 128:256]


def _inproj(x, mod_l, row_of_tile, wqkv, wkt, wfp, wifh, wifl, bif):
    n = x.shape[0]
    const = lambda i: (0, 0)
    return pl.pallas_call(
        _inproj_kernel,
        grid=(n // TM,),
        in_specs=[
            pl.BlockSpec((TM, D), lambda i: (i, 0)),
            pl.BlockSpec((None, 6, D), lambda i: (row_of_tile(i), 0, 0)),
            pl.BlockSpec((D, 3 * A_W), const),
            pl.BlockSpec((A_W, D), const),
            pl.BlockSpec((D, B_W + C_W), const),
            pl.BlockSpec((D, 256), const),
            pl.BlockSpec((D, 256), const),
            pl.BlockSpec((1, 256), const),
        ],
        out_specs=[
            pl.BlockSpec((TM, 3 * A_W), lambda i: (i, 0)),
            pl.BlockSpec((A_W, TM), lambda i: (0, i)),
            pl.BlockSpec((TM, B_W + C_W), lambda i: (i, 0)),
            pl.BlockSpec((TM, 128), lambda i: (i, 0)),
            pl.BlockSpec((TM, 128), lambda i: (i, 0)),
        ],
        out_shape=[
            jax.ShapeDtypeStruct((n, 3 * A_W), BF16),
            jax.ShapeDtypeStruct((A_W, n), BF16),
            jax.ShapeDtypeStruct((n, B_W + C_W), F32),
            jax.ShapeDtypeStruct((n, 128), F32),
            jax.ShapeDtypeStruct((n, 128), F32),
        ],
        compiler_params=_params(("arbitrary",)),
        name="inproj",
    )(x, mod_l, wqkv, wkt, wfp, wifh, wifl, bif)


def _gates_kernel(gi_ref, gf_ref, git_ref, gft_ref, tri_ref, trit_ref,
                  bcol_ref, rcol_ref, rrow_ref, gsum_ref, rmx_ref):
    tri = tri_ref[...]
    trit = trit_ref[...]
    h, m, l = _split3(_log_sigmoid(gf_ref[...]))
    cum_f = _dot(tri, h) + _dot(tri, m) + _dot(tri, l)
    cum_b = _dot(trit, h) + _dot(trit, m) + _dot(trit, l)
    lane = lax.broadcasted_iota(jnp.int32, (CH, 128), 1)
    bcol = jnp.where(lane < HEADS, cum_f, cum_b)
    bcol_ref[...] = bcol
    rcol_ref[...] = gi_ref[...] - bcol

    h, m, l = _split3(_log_sigmoid(gft_ref[...]))
    row_f = _dot(h, trit) + _dot(m, trit) + _dot(l, trit)
    row_b = _dot(h, tri) + _dot(m, tri) + _dot(l, tri)
    row = lax.broadcasted_iota(jnp.int32, (8, CH), 0)
    brow = jnp.where(row < HEADS, row_f, row_b)
    rrow = git_ref[...] - brow
    rrow_ref[...] = rrow
    row1 = lax.broadcasted_iota(jnp.int32, (8, 1), 0)
    total = jnp.where(row1 < HEADS, brow[:, CH - 1:CH], brow[:, 0:1])
    gsum_ref[...] = jnp.broadcast_to(total, (8, 128))
    rmx_ref[...] = jnp.broadcast_to(jnp.max(rrow, axis=1, keepdims=True), (8, 128))


def _gates(gi, gf, git, gft, tri, trit):
    n = gi.shape[0]
    nch = n // CH
    const = lambda i: (0, 0)
    return pl.pallas_call(
        _gates_kernel,
        grid=(nch,),
        in_specs=[
            pl.BlockSpec((CH, 128), lambda i: (i, 0)),
            pl.BlockSpec((CH, 128), lambda i: (i, 0)),
            pl.BlockSpec((8, CH), lambda i: (0, i)),
            pl.BlockSpec((8, CH), lambda i: (0, i)),
            pl.BlockSpec((CH, CH), const),
            pl.BlockSpec((CH, CH), const),
        ],
        out_specs=[
            pl.BlockSpec((CH, 128), lambda i: (i, 0)),
            pl.BlockSpec((CH, 128), lambda i: (i, 0)),
            pl.BlockSpec((8, CH), lambda i: (0, i)),
            pl.BlockSpec((None, 8, 128), lambda i: (i, 0, 0)),
            pl.BlockSpec((None, 8, 128), lambda i: (i, 0, 0)),
        ],
        out_shape=[
            jax.ShapeDtypeStruct((n, 128), F32),
            jax.ShapeDtypeStruct((n, 128), F32),
            jax.ShapeDtypeStruct((8, n), F32),
            jax.ShapeDtypeStruct((nch, 8, 128), F32),
            jax.ShapeDtypeStruct((nch, 8, 128), F32),
        ],
        compiler_params=_params(("arbitrary",)),
        name="gates",
    )(gi, gf, git, gft, tri, trit)


def _mlstm_kernel(*refs, nchunk, has_state):
    if has_state:
        (q_ref, k_ref, v_ref, kt_ref, bcol_ref, rcol_ref, rrow_ref, gsum_ref, rmx_ref, lng_ref,
         cn0_ref, m0_ref, hn_ref, cn_st) = refs
    else:
        (q_ref, k_ref, v_ref, kt_ref, bcol_ref, rcol_ref, rrow_ref, gsum_ref, rmx_ref, lng_ref,
         hn_ref, cn_out_ref, m_out_ref) = refs
    head = pl.program_id(1)
    lane = lax.broadcasted_iota(jnp.int32, (CH, 128), 1)
    tpos = lax.broadcasted_iota(jnp.int32, (CH, CH), 0)
    spos = lax.broadcasted_iota(jnp.int32, (CH, CH), 1)
    masks = (spos <= tpos, spos >= tpos)

    def col(ref, j, idx):
        blk = ref[j * CH:(j + 1) * CH, :]
        return jnp.sum(jnp.where(lane == idx, blk, 0.0), axis=1, keepdims=True)

    m_in = [[None] * nchunk, [None] * nchunk]
    for d in range(2):
        idx = d * HEADS + head
        cn = cn0_ref[d] if has_state else None
        m = m0_ref[pl.ds(idx, 1), 0:1] if has_state else None
        order = range(nchunk) if d == 0 else range(nchunk - 1, -1, -1)
        for pos_in_sweep, j in enumerate(order):
            if m is not None:
                cn_st[d, j] = cn
                m_in[d][j] = m
            if has_state and pos_in_sweep == nchunk - 1:
                break
            total = gsum_ref[j, pl.ds(idx, 1), 0:1]
            rmax = rmx_ref[j, pl.ds(idx, 1), 0:1]
            mm = rmax if m is None else jnp.maximum(m, rmax)
            ws = jnp.exp(col(rcol_ref, j, idx) - mm)
            vj = v_ref[j * CH:(j + 1) * CH, :].astype(F32)
            vs = jnp.concatenate([(vj * ws).astype(BF16),
                                  jnp.where(lane == 0, ws, 0.0).astype(BF16)], axis=1)
            upd = _dot(kt_ref[:, j * CH:(j + 1) * CH], vs)
            cn = upd if m is None else jnp.exp(m - mm) * cn + upd
            m = total + mm
        if not has_state:
            cn_out_ref[d] = cn
            m_out_ref[d:d + 1, :] = jnp.broadcast_to(m, (1, 128))

    for j in range(nchunk):
        sl = slice(j * CH, (j + 1) * CH)
        qj = q_ref[sl, :]
        s_raw = _dot_nt(qj, k_ref[sl, :])
        p_sum, h_inter = None, None
        for d in range(2):
            idx = d * HEADS + head
            rr = rrow_ref[pl.ds(idx, 1), sl]
            m_row = jnp.max(jnp.where(masks[d], rr, -jnp.inf), axis=1, keepdims=True)
            m_prev = m_in[d][j]
            if m_prev is not None:
                m_row = jnp.maximum(m_row, m_prev)
            p = s_raw * jnp.where(masks[d], jnp.exp(rr - m_row), 0.0)
            den = jnp.sum(p, axis=1, keepdims=True)
            if m_prev is not None:
                inter = jnp.exp(m_prev - m_row)
                qc = _dot(qj, cn_st[d, j].astype(BF16))
                den = den + inter * qc[:, DH:DH + 1]
            floor = jnp.exp(-(col(bcol_ref, j, idx) + m_row))
            rinv = 1.0 / jnp.maximum(jnp.abs(den), floor)
            p_sum = p * rinv if p_sum is None else p_sum + p * rinv
            if m_prev is not None:
                hi = (inter * rinv) * qc[:, 0:DH]
                h_inter = hi if h_inter is None else h_inter + hi
        hj = _dot(p_sum.astype(BF16), v_ref[sl, :])
        if h_inter is not None:
            hj = hj + h_inter
        mu = jnp.mean(hj, axis=-1, keepdims=True)
        hc = hj - mu
        var = jnp.mean(hc * hc, axis=-1, keepdims=True)
        hn_ref[sl, :] = hc * lax.rsqrt(var + LN_EPS) * lng_ref[...]


def _mlstm(qkv, kt, bcol, rcol, rrow, gsum, rmx, lng, tok0, nseq, t, cn0=None, m0=None):
    nchunk = t // CH
    sb = tok0 // t
    cb = tok0 // CH
    has_state = cn0 is not None
    in_specs = [
        pl.BlockSpec((t, DH), lambda s, h: (sb + s, h)),
        pl.BlockSpec((t, DH), lambda s, h: (sb + s, HEADS + h)),
        pl.BlockSpec((t, DH), lambda s, h: (sb + s, 2 * HEADS + h)),
        pl.BlockSpec((DH, t), lambda s, h: (h, sb + s)),
        pl.BlockSpec((t, 128), lambda s, h: (sb + s, 0)),
        pl.BlockSpec((t, 128), lambda s, h: (sb + s, 0)),
        pl.BlockSpec((8, t), lambda s, h: (0, sb + s)),
        pl.BlockSpec((nchunk, 8, 128), lambda s, h: (cb // nchunk + s, 0, 0)),
        pl.BlockSpec((nchunk, 8, 128), lambda s, h: (cb // nchunk + s, 0, 0)),
        pl.BlockSpec((1, DH), lambda s, h: (0, h)),
    ]
    args = [qkv, qkv, qkv, kt, bcol, rcol, rrow, gsum, rmx, lng]
    hn_spec = pl.BlockSpec((t, DH), lambda s, h: (s, h))
    hn_shape = jax.ShapeDtypeStruct((nseq * t, A_W), F32)
    if has_state:
        in_specs += [
            pl.BlockSpec((None, 2, None, DH, 2 * DH), lambda s, h: (s, 0, h, 0, 0)),
            pl.BlockSpec((None, 8, 128), lambda s, h: (s, 0, 0)),
        ]
        args += [cn0, m0]
        out_specs, out_shape = hn_spec, hn_shape
        scratch = [pltpu.VMEM((2, nchunk, DH, 2 * DH), F32)]
    else:
        out_specs = [
            hn_spec,
            pl.BlockSpec((None, 2, None, DH, 2 * DH), lambda s, h: (s, 0, h, 0, 0)),
            pl.BlockSpec((None, None, 2, 128), lambda s, h: (s, h, 0, 0)),
        ]
        out_shape = [
            hn_shape,
            jax.ShapeDtypeStruct((nseq, 2, HEADS, DH, 2 * DH), F32),
            jax.ShapeDtypeStruct((nseq, HEADS, 2, 128), F32),
        ]
        scratch = []
    return pl.pallas_call(
        functools.partial(_mlstm_kernel, nchunk=nchunk, has_state=has_state),
        grid=(nseq, HEADS),
        in_specs=in_specs,
        out_specs=out_specs,
        out_shape=out_shape,
        scratch_shapes=scratch,
        compiler_params=_params(("arbitrary", "arbitrary")),
        name="mlstm_lat" if has_state else "mlstm_ctx",
    )(*args)


def _fftpool_kernel(fp_ref, cth_ref, ctl_ref, sth_ref, stl_ref, bcsh_ref, bcsl_ref, icnt_ref,
                    wpool_ref, pscale_ref, yf_ref, pp_ref, pad_ref, *, t):
    fh, fl = _split2(fp_ref[:, 0:B_W])
    xcs = _dot(fh, bcsh_ref[...]) + _dot(fl, bcsh_ref[...]) + _dot(fh, bcsl_ref[...])
    xch, xcl = _split2(xcs[:, 0:B_W])
    xsh, xsl = _split2(xcs[:, B_W:2 * B_W])
    y = (_dot(cth_ref[...], xch) + _dot(ctl_ref[...], xch) + _dot(cth_ref[...], xcl)
         - _dot(sth_ref[...], xsh) - _dot(stl_ref[...], xsh) - _dot(sth_ref[...], xsl))
    yf_ref[...] = (y * ((t * B_GW) ** -0.5)).astype(BF16)

    p = fp_ref[:, B_W:B_W + C_W]
    zero = jnp.zeros((POOL_HALO, C_W), F32)
    pad_ref[0:POOL_HALO, :] = zero
    pad_ref[POOL_HALO + t:2 * POOL_HALO + t, :] = zero
    pad_ref[POOL_HALO:POOL_HALO + t, :] = p

    def sh(dlt):
        return pad_ref[POOL_HALO + dlt:POOL_HALO + dlt + t, :]

    w2 = sh(-1) + p
    w4 = w2 + sh(-2) + sh(1)
    w8 = w4 + sh(-4) + sh(-3) + sh(2) + sh(3)
    w16 = w8 + sh(-8) + sh(-7) + sh(-6) + sh(-5) + sh(4) + sh(5) + sh(6) + sh(7)
    grp = lax.broadcasted_iota(jnp.int32, (t, C_W), 1) // C_GW
    wsum = jnp.where(grp == 0, w2, jnp.where(grp == 1, w4, jnp.where(grp == 2, w8, w16)))
    pooled = wsum * icnt_ref[...] - p
    pp_ref[...] = (_dot(pooled.astype(BF16), wpool_ref[...]) * pscale_ref[...]).astype(BF16)


def _fftpool(fp, consts, wpool_bd, pscale, tok0, nseq, t):
    cth, ctl, sth, stl, bcsh, bcsl, icnt = consts
    sb = tok0 // t
    const = lambda s: (0, 0)
    return pl.pallas_call(
        functools.partial(_fftpool_kernel, t=t),
        grid=(nseq,),
        in_specs=[
            pl.BlockSpec((t, B_W + C_W), lambda s: (sb + s, 0)),
            pl.BlockSpec((t, t), const), pl.BlockSpec((t, t), const),
            pl.BlockSpec((t, t), const), pl.BlockSpec((t, t), const),
            pl.BlockSpec((B_W, 2 * B_W), const), pl.BlockSpec((B_W, 2 * B_W), const),
            pl.BlockSpec((t, C_W), const),
            pl.BlockSpec((C_W, C_W), const),
            pl.BlockSpec((1, C_W), const),
        ],
        out_specs=[pl.BlockSpec((t, B_W), lambda s: (s, 0)), pl.BlockSpec((t, C_W), lambda s: (s, 0))],
        out_shape=[jax.ShapeDtypeStruct((nseq * t, B_W), BF16), jax.ShapeDtypeStruct((nseq * t, C_W), BF16)],
        scratch_shapes=[pltpu.VMEM((t + 2 * POOL_HALO, C_W), F32)],
        compiler_params=_params(("arbitrary",)),
        name=f"fftpool_{t}",
    )(fp, cth, ctl, sth, stl, bcsh, bcsl, icnt, wpool_bd, pscale)


def _merge_kernel(x_ref, mod_ref, hn_ref, yf_ref, pp_ref, wo_ref, wg_ref, wpa_ref, wpb_ref, wpc_ref,
                  wout_ref, g1_ref, b1_ref, wrh_ref, wrl_ref, br_ref, x1_ref, u2_ref, route_ref):
    x = x_ref[...]
    ub = (x * (1.0 + mod_ref[1:2, :]) + mod_ref[0:1, :]).astype(BF16)
    a = (hn_ref[...] * _sigmoid(_dot(ub, wo_ref[...]))).astype(BF16)
    acc = _sigmoid(_dot(ub, wg_ref[:, 0:D])) * _dot(a, wpa_ref[...])
    acc = acc + _sigmoid(_dot(ub, wg_ref[:, D:2 * D])) * _dot(yf_ref[...], wpb_ref[...])
    acc = acc + _sigmoid(_dot(ub, wg_ref[:, 2 * D:3 * D])) * _dot(pp_ref[...], wpc_ref[...])
    out = _dot(acc.astype(BF16), wout_ref[...])
    x1 = _layer_norm(ALPHA * x + mod_ref[2:3, :] * out, g1_ref[...], b1_ref[...])
    x1_ref[...] = x1
    u2 = x1 * (1.0 + mod_ref[4:5, :]) + mod_ref[3:4, :]
    u2_ref[...] = u2.astype(BF16)

    uh, ul = _split2(u2)
    lt = _dot_nt(wrh_ref[...], uh) + _dot_nt(wrh_ref[...], ul) + _dot_nt(wrl_ref[...], uh) + br_ref[:, 0:1]
    gl = [lt[g:g + 1, :] for g in range(N_GROUPS)]
    gmax = jnp.maximum(jnp.maximum(gl[0], gl[1]), jnp.maximum(gl[2], gl[3]))
    gidx = jnp.where(gl[0] >= gmax, 0, jnp.where(gl[1] >= gmax, 1, jnp.where(gl[2] >= gmax, 2, 3)))
    gval = 1.0 / (jnp.exp(gl[0] - gmax) + jnp.exp(gl[1] - gmax) + jnp.exp(gl[2] - gmax) + jnp.exp(gl[3] - gmax))
    fe = []
    for e in range(EXP_PER_GROUP):
        r = [lt[N_GROUPS + g * EXP_PER_GROUP + e:N_GROUPS + g * EXP_PER_GROUP + e + 1, :] for g in range(N_GROUPS)]
        fe.append(jnp.where(gidx == 0, r[0], jnp.where(gidx == 1, r[1], jnp.where(gidx == 2, r[2], r[3]))))

    def top1(v):
        vmax = jnp.maximum(jnp.maximum(v[0], v[1]), jnp.maximum(v[2], v[3]))
        imax = jnp.where(v[0] >= vmax, 0, jnp.where(v[1] >= vmax, 1, jnp.where(v[2] >= vmax, 2, 3)))
        return vmax, imax

    v1, i1 = top1(fe)
    v2, i2 = top1([jnp.where(i1 == e, -jnp.inf, fe[e]) for e in range(EXP_PER_GROUP)])
    tt = jnp.exp(v2 - v1)
    w1 = gval / (1.0 + tt)
    w2 = gval * tt / (1.0 + tt)
    e1 = (gidx * EXP_PER_GROUP + i1).astype(F32)
    e2 = (gidx * EXP_PER_GROUP + i2).astype(F32)
    route_ref[...] = jnp.zeros(route_ref.shape, F32)
    route_ref[0:1, :] = e1
    route_ref[1:2, :] = e2
    route_ref[2:3, :] = w1
    route_ref[3:4, :] = w2


def _merge(x, mod_l, row_of_tile, hn, yf, pp, wo, wg, wpa, wpb, wpc, wout, g1, b1, wrh, wrl, br):
    n = x.shape[0]
    const = lambda i: (0, 0)
    tok = lambda i: (i, 0)
    return pl.pallas_call(
        _merge_kernel,
        grid=(n // TM,),
        in_specs=[
            pl.BlockSpec((TM, D), tok),
            pl.BlockSpec((None, 6, D), lambda i: (row_of_tile(i), 0, 0)),
            pl.BlockSpec((TM, A_W), tok),
            pl.BlockSpec((TM, B_W), tok),
            pl.BlockSpec((TM, C_W), tok),
            pl.BlockSpec((D, A_W), const),
            pl.BlockSpec((D, 3 * D), const),
            pl.BlockSpec((A_W, D), const),
            pl.BlockSpec((B_W, D), const),
            pl.BlockSpec((C_W, D), const),
            pl.BlockSpec((D, D), const),
            pl.BlockSpec((1, D), const),
            pl.BlockSpec((1, D), const),
            pl.BlockSpec((32, D), const),
            pl.BlockSpec((32, D), const),
            pl.BlockSpec((32, 128), const),
        ],
        out_specs=[pl.BlockSpec((TM, D), tok), pl.BlockSpec((TM, D), tok), pl.BlockSpec((8, TM), lambda i: (0, i))],
        out_shape=[jax.ShapeDtypeStruct((n, D), F32), jax.ShapeDtypeStruct((n, D), BF16),
                   jax.ShapeDtypeStruct((8, n), F32)],
        compiler_params=_params(("arbitrary",)),
        name="merge",
    )(x, mod_l, hn, yf, pp, wo, wg, wpa, wpb, wpc, wout, g1, b1, wrh, wrl, br)


def _lsort_kernel(route_ref, u2_ref, upper_ref, lower_ref, xl_ref, wl_ref, cnt_ref, loc_ref):
    e0 = route_ref[0:1, :].astype(jnp.int32)
    e1 = route_ref[1:2, :].astype(jnp.int32)
    eid = lax.broadcasted_iota(jnp.int32, (EP, TM), 0)
    m0 = eid == e0
    m1 = eid == e1
    member = jnp.where(m0 | m1, 1.0, 0.0)
    rank = _dot(member.astype(BF16), upper_ref[...])
    cnt = jnp.sum(member, axis=1, keepdims=True)
    nwin = jnp.floor((cnt + (W_ROWS - 1)) * (1.0 / W_ROWS))
    first = _dot(lower_ref[...], jnp.broadcast_to(nwin, (EP, 128)).astype(BF16))[:, 0:1] * W_ROWS
    r0 = jnp.sum(jnp.where(m0, first + rank, 0.0), axis=0, keepdims=True).astype(jnp.int32)
    r1 = jnp.sum(jnp.where(m1, first + rank, 0.0), axis=0, keepdims=True).astype(jnp.int32)
    rid = lax.broadcasted_iota(jnp.int32, (RL, TM), 0)
    h0 = rid == r0
    h1 = rid == r1
    xl_ref[...] = _dot(jnp.where(h0 | h1, 1.0, 0.0).astype(BF16), u2_ref[...])
    wsel = jnp.where(h0, route_ref[2:3, :], 0.0) + jnp.where(h1, route_ref[3:4, :], 0.0)
    wl_ref[...] = jnp.broadcast_to(jnp.sum(wsel, axis=1, keepdims=True), (RL, 128))
    cnt_ref[...] = jnp.broadcast_to(cnt[0:N_EXPERTS, :], (N_EXPERTS, 128))
    loc_ref[...] = jnp.zeros(loc_ref.shape, F32)
    loc_ref[0:1, :] = r0.astype(F32)
    loc_ref[1:2, :] = r1.astype(F32)


def _lsort(route, u2, upper, lower):
    n = u2.shape[0]
    nb = n // TM
    const = lambda i: (0, 0)
    return pl.pallas_call(
        _lsort_kernel,
        grid=(nb,),
        in_specs=[
            pl.BlockSpec((8, TM), lambda i: (0, i)),
            pl.BlockSpec((TM, D), lambda i: (i, 0)),
            pl.BlockSpec((TM, TM), const),
            pl.BlockSpec((EP, EP), const),
        ],
        out_specs=[
            pl.BlockSpec((RL, D), lambda i: (i, 0)),
            pl.BlockSpec((RL, 128), lambda i: (i, 0)),
            pl.BlockSpec((None, N_EXPERTS, 128), lambda i: (i, 0, 0)),
            pl.BlockSpec((8, TM), lambda i: (0, i)),
        ],
        out_shape=[
            jax.ShapeDtypeStruct((nb * RL, D), F32),
            jax.ShapeDtypeStruct((nb * RL, 128), F32),
            jax.ShapeDtypeStruct((nb, N_EXPERTS, 128), F32),
            jax.ShapeDtypeStruct((8, n), F32),
        ],
        compiler_params=_params(("arbitrary",)),
        name="lsort",
    )(route, u2, upper, lower)


def _window_loop(make_copies, n_windows, start):
    def body(w, c):
        for cp in make_copies(w):
            if start:
                cp.start()
            else:
                cp.wait()
        return c

    lax.fori_loop(0, n_windows, body, 0)


def _experts_kernel(wsrc_ref, te_ref, nused_ref, xl_hbm, wl_hbm, wup_ref, wdn_ref, ys_ref, xbuf, wbuf, sems):
    j = pl.program_id(0)
    nused = nused_ref[0]
    slot = j % 2

    def copies(tile, slot_):
        def make(w):
            src = pl.ds(pl.multiple_of(wsrc_ref[tile * WPT + w], W_ROWS), W_ROWS)
            dst = pl.ds(pl.multiple_of(w * W_ROWS, W_ROWS), W_ROWS)
            return (pltpu.make_async_copy(xl_hbm.at[src], xbuf.at[slot_, dst], sems.at[0, slot_]),
                    pltpu.make_async_copy(wl_hbm.at[src], wbuf.at[slot_, dst], sems.at[1, slot_]))
        return make

    @pl.when(j == 0)
    def _():
        _window_loop(copies(0, 0), WPT, True)

    @pl.when(j + 1 < nused)
    def _():
        _window_loop(copies(j + 1, 1 - slot), WPT, True)

    @pl.when(j < nused)
    def _():
        _window_loop(copies(j, slot), WPT, False)
        hid = _dot(xbuf[slot].astype(BF16), wup_ref[...].astype(BF16))
        a = hid[:, 0:D_EXPERT]
        act = a * _sigmoid(a) * hid[:, D_EXPERT:2 * D_EXPERT] * wbuf[slot][:, 0:1]
        ys_ref[...] = _dot(act.astype(BF16), wdn_ref[...].astype(BF16))

    @pl.when(j >= nused)
    def _():
        ys_ref[...] = jnp.zeros(ys_ref.shape, F32)


def _experts(wsrc, tile_expert, nused, xl, wl, w_up_l, w_down_l, nt):
    return pl.pallas_call(
        _experts_kernel,
        grid_spec=pltpu.PrefetchScalarGridSpec(
            num_scalar_prefetch=3,
            grid=(nt,),
            in_specs=[
                pl.BlockSpec(memory_space=pl.ANY),
                pl.BlockSpec(memory_space=pl.ANY),
                pl.BlockSpec((None, D, 2 * D_EXPERT), lambda i, ws, te, nu: (te[i], 0, 0)),
                pl.BlockSpec((None, D_EXPERT, D), lambda i, ws, te, nu: (te[i], 0, 0)),
            ],
            out_specs=pl.BlockSpec((TE, D), lambda i, ws, te, nu: (i, 0)),
            scratch_shapes=[pltpu.VMEM((2, TE, D), F32), pltpu.VMEM((2, TE, 128), F32),
                            pltpu.SemaphoreType.DMA((2, 2))],
        ),
        out_shape=jax.ShapeDtypeStruct((nt * TE, D), F32),
        compiler_params=_params(("arbitrary",)),
        name="experts",
    )(wsrc, tile_expert, nused, xl, wl, w_up_l, w_down_l)


def _combine_kernel(gw_ref, ys_hbm, x1_ref, mod_ref, loc_ref, g2_ref, b2_ref, o_ref, ybuf, sems):
    i = pl.program_id(0)
    slot = i % 2

    def copies(blk, slot_):
        def make(s):
            src = pl.ds(pl.multiple_of(gw_ref[blk * NSLOT + s], W_ROWS), W_ROWS)
            dst = pl.ds(pl.multiple_of(s * W_ROWS, W_ROWS), W_ROWS)
            return (pltpu.make_async_copy(ys_hbm.at[src], ybuf.at[slot_, dst], sems.at[slot_]),)
        return make

    @pl.when(i == 0)
    def _():
        _window_loop(copies(0, 0), NSLOT, True)

    @pl.when(i + 1 < pl.num_programs(0))
    def _():
        _window_loop(copies(i + 1, 1 - slot), NSLOT, True)

    _window_loop(copies(i, slot), NSLOT, False)
    r0 = loc_ref[:, 0:1].astype(jnp.int32)
    r1 = loc_ref[:, 1:2].astype(jnp.int32)
    lid = lax.broadcasted_iota(jnp.int32, (TM, RL), 1)
    pt = jnp.where((lid == r0) | (lid == r1), 1.0, 0.0).astype(BF16)
    yh, yl = _split2(ybuf[slot])
    y = _dot(pt, yh) + _dot(pt, yl)
    o_ref[...] = _layer_norm(ALPHA * x1_ref[...] + mod_ref[5:6, :] * y, g2_ref[...], b2_ref[...])


def _combine(gw, ys, x1, mod_l, row_of_tile, loc_t, g2, b2):
    n = x1.shape[0]
    return pl.pallas_call(
        _combine_kernel,
        grid_spec=pltpu.PrefetchScalarGridSpec(
            num_scalar_prefetch=1,
            grid=(n // TM,),
            in_specs=[
                pl.BlockSpec(memory_space=pl.ANY),
                pl.BlockSpec((TM, D), lambda i, gw: (i, 0)),
                pl.BlockSpec((None, 6, D), lambda i, gw: (row_of_tile(i), 0, 0)),
                pl.BlockSpec((TM, 8), lambda i, gw: (i, 0)),
                pl.BlockSpec((1, D), lambda i, gw: (0, 0)),
                pl.BlockSpec((1, D), lambda i, gw: (0, 0)),
            ],
            out_specs=pl.BlockSpec((TM, D), lambda i, gw: (i, 0)),
            scratch_shapes=[pltpu.VMEM((2, RL, D), F32), pltpu.SemaphoreType.DMA((2,))],
        ),
        out_shape=jax.ShapeDtypeStruct((n, D), F32),
        compiler_params=_params(("arbitrary",)),
        name="combine",
    )(gw, ys, x1, mod_l, loc_t, g2, b2)


def _routing_tables(cnt, nt):
    nb = cnt.shape[0]
    nwin = (cnt + W_ROWS - 1) // W_ROWS
    lo_inc = jnp.cumsum(nwin, axis=1)
    lo = lo_inc - nwin
    bo_inc = jnp.cumsum(nwin, axis=0)
    bo = bo_inc - nwin
    we = bo_inc[-1]
    wpad = ((we + WPT - 1) // WPT) * WPT
    eend = jnp.cumsum(wpad)
    eoff = eend - wpad
    nused = (eend[-1] // WPT).reshape(1)
    tile_expert = jnp.minimum(
        jnp.sum((jnp.arange(nt, dtype=jnp.int32)[:, None] * WPT >= eend[None, :]).astype(jnp.int32), axis=1),
        N_EXPERTS - 1)
    g = jnp.arange(nt * WPT, dtype=jnp.int32)
    eg = tile_expert[g // WPT]
    ig = g - eoff[eg]
    bg = jnp.minimum(jnp.sum((bo_inc[:, eg].T <= ig[:, None]).astype(jnp.int32), axis=1), nb - 1)
    empty_window_row = RL - W_ROWS
    wsrc = jnp.where(ig < we[eg], bg * RL + (lo[bg, eg] + ig - bo[bg, eg]) * W_ROWS, empty_window_row)
    s = jnp.arange(NSLOT, dtype=jnp.int32)
    es = jnp.minimum(jnp.sum((lo_inc[:, None, :] <= s[None, :, None]).astype(jnp.int32), axis=2), N_EXPERTS - 1)
    take = lambda a: jnp.take_along_axis(a, es, axis=1)
    gwin = eoff[es] + take(bo) + s[None, :] - take(lo)
    gw = jnp.where(s[None, :] < lo_inc[:, -1:], gwin, 0) * W_ROWS
    i32 = lambda a: a.astype(jnp.int32)
    return i32(wsrc), i32(tile_expert), i32(nused), i32(gw).reshape(-1)


def _grid_pos_embed(n_tok):
    rows = n_tok // GRID_W
    r, col = np.meshgrid(np.arange(rows, dtype=np.float32), np.arange(GRID_W, dtype=np.float32), indexing="ij")
    r, col = r.reshape(-1), col.reshape(-1)
    quarter = D // 4
    freq = (1.0 / (np.float32(POS_BASE) ** (np.arange(quarter, dtype=np.float32) / np.float32(quarter)))).astype(np.float32)
    ang_r = r[:, None] * freq[None, :]
    ang_c = col[:, None] * freq[None, :]
    return jnp.asarray(np.concatenate([np.sin(ang_r), np.cos(ang_r), np.sin(ang_c), np.cos(ang_c)], axis=-1),
                       dtype=F32)


def _dft_consts(t):
    def hl(a):
        hi = a.astype(BF16)
        lo = (a - hi.astype(np.float64)).astype(BF16)
        return jnp.asarray(hi), jnp.asarray(lo)

    tt = np.arange(t)
    ang = 2.0 * np.pi * ((tt[:, None] * tt[None, :]) % t) / t
    cth, ctl = hl(np.cos(ang))
    sth, stl = hl(np.sin(ang))
    cc = np.arange(B_GW)
    angc = 2.0 * np.pi * ((cc[:, None] * cc[None, :]) % B_GW) / B_GW
    eye = np.eye(B_W // B_GW)
    bcs = np.concatenate([np.kron(eye, np.cos(angc)), np.kron(eye, np.sin(angc))], axis=1)
    bcsh, bcsl = hl(bcs)
    pos = np.arange(t)
    icnt = np.concatenate(
        [np.repeat((1.0 / (np.minimum(pos + w // 2, t) - np.maximum(pos - w // 2, 0)))[:, None], C_GW, axis=1)
         for w in POOL_WINDOWS], axis=1)
    return cth, ctl, sth, stl, bcsh, bcsl, jnp.asarray(icnt, F32)


def kernel(x_prompt, x_sample, c, state_C, state_n, state_m, c_ctx, ln_in_g, ln_in_b, w_mod, b_mod, w_in, b_if, ln_a_g, w_pa, w_pb, w_pc, w_pool, pool_scale, w_out, ln1_g, ln1_b, w_r1, b_r1, w_r2, b_r2, w_up, w_down, ln2_g, ln2_b):
    bc, tc, _ = x_prompt.shape
    bl, tl, _ = x_sample.shape
    nc, nl = bc * tc, bl * tl
    n = nc + nl
    assert tc == CH and tl % CH == 0 and nc % tl == 0 and bl + 1 <= N_MOD_ROWS
    nt = -(-(2 * n // W_ROWS + (n // TM) * N_EXPERTS + N_EXPERTS * (WPT - 1)) // WPT)
    upper = jnp.asarray(np.triu(np.ones((TM, TM), np.float32), 1), BF16)
    lower = jnp.asarray(np.tril(np.ones((EP, EP), np.float32), -1), BF16)

    def row_of_tile(i):
        return jnp.where(i < nc // TM, 0, 1 + jnp.maximum(i - nc // TM, 0) // (tl // TM))

    wqkv = w_in[:, :, 0:3 * A_W].astype(BF16)
    wkt = jnp.swapaxes(w_in[:, :, A_W:2 * A_W], 1, 2).astype(BF16)
    wo = w_in[:, :, 3 * A_W:4 * A_W].astype(BF16)
    c0 = 4 * A_W
    wif = w_in[:, :, c0:c0 + 16].reshape(DEPTH, D, 2, 2, HEADS)
    zpad = jnp.zeros((DEPTH, D, 128 - 2 * HEADS), F32)
    wif = jnp.concatenate([wif[:, :, :, 0, :].reshape(DEPTH, D, 2 * HEADS), zpad,
                           wif[:, :, :, 1, :].reshape(DEPTH, D, 2 * HEADS), zpad], axis=2)
    wifh, wifl = _split2(wif)
    bpad = jnp.zeros((DEPTH, 128 - 2 * HEADS), F32)
    bif = jnp.concatenate([b_if[:, :, 0, :].reshape(DEPTH, 2 * HEADS), bpad,
                           b_if[:, :, 1, :].reshape(DEPTH, 2 * HEADS), bpad], axis=1).reshape(DEPTH, 1, 256)
    wfp = w_in[:, :, c0 + 16:c0 + 16 + B_W + C_W].astype(BF16)
    wg = w_in[:, :, c0 + 16 + B_W + C_W:].astype(BF16)
    wpa, wpb, wpc, wout = (w.astype(BF16) for w in (w_pa, w_pb, w_pc, w_out))
    eye = jnp.eye(C_W // C_GW, dtype=F32)
    wpool_bd = jnp.einsum("gh,lgce->lgche", eye, w_pool).reshape(DEPTH, C_W, C_W).astype(BF16)
    wr = jnp.concatenate([w_r1, w_r2.reshape(DEPTH, D, N_EXPERTS),
                          jnp.zeros((DEPTH, D, 32 - N_GROUPS - N_EXPERTS), F32)], axis=2)
    wrh, wrl = _split2(jnp.swapaxes(wr, 1, 2))
    br = jnp.concatenate([b_r1, b_r2.reshape(DEPTH, N_EXPERTS),
                          jnp.zeros((DEPTH, 32 - N_GROUPS - N_EXPERTS), F32)], axis=1)
    br = jnp.broadcast_to(br[:, :, None], (DEPTH, 32, 128))

    tri_np = np.tril(np.ones((CH, CH), np.float32))
    tri, trit = jnp.asarray(tri_np, BF16), jnp.asarray(tri_np.T, BF16)
    consts_c, consts_l = _dft_consts(tc), _dft_consts(tl)

    cond = jnp.concatenate([c_ctx[None, :], c, jnp.zeros((N_MOD_ROWS - 1 - bl, D), F32)], axis=0)
    mod = _mod_all(cond, w_mod, b_mod).reshape(DEPTH, N_MOD_ROWS, 6, D)

    x = _ln_in(x_prompt.reshape(nc, D), x_sample.reshape(nl, D), _grid_pos_embed(tl),
               ln_in_g.reshape(1, D), ln_in_b.reshape(1, D), tl)

    cn0 = jnp.concatenate([state_C, state_n[..., None], jnp.zeros(state_C.shape[:-1] + (DH - 1,), F32)], axis=-1)
    m0 = jnp.broadcast_to(state_m.reshape(bl, DEPTH, 2 * HEADS, 1), (bl, DEPTH, 2 * HEADS, 128))

    new_c, new_n, new_m = [], [], []
    for l in range(DEPTH):
        mod_l = mod[l]
        qkv, kt, fp, gi, gf = _inproj(x, mod_l, row_of_tile, wqkv[l], wkt[l], wfp[l], wifh[l], wifl[l], bif[l])
        bcol, rcol, rrow, gsum, rmx = _gates(gi, gf, gi[:, 0:8].T, gf[:, 0:8].T, tri, trit)
        lng = ln_a_g[l].reshape(1, A_W)
        hn_c, cn_c, m_c = _mlstm(qkv, kt, bcol, rcol, rrow, gsum, rmx, lng, 0, bc, tc)
        hn_l = _mlstm(qkv, kt, bcol, rcol, rrow, gsum, rmx, lng, nc, bl, tl, cn0[:, l], m0[:, l])
        pscale = pool_scale[l].reshape(1, C_W)
        yf_c, pp_c = _fftpool(fp, consts_c, wpool_bd[l], pscale, 0, bc, tc)
        yf_l, pp_l = _fftpool(fp, consts_l, wpool_bd[l], pscale, nc, bl, tl)
        hn = jnp.concatenate([hn_c, hn_l], axis=0)
        yf = jnp.concatenate([yf_c, yf_l], axis=0)
        pp = jnp.concatenate([pp_c, pp_l], axis=0)
        x1, u2, route = _merge(x, mod_l, row_of_tile, hn, yf, pp, wo[l], wg[l], wpa[l], wpb[l], wpc[l], wout[l],
                               ln1_g[l].reshape(1, D), ln1_b[l].reshape(1, D), wrh[l], wrl[l], br[l])
        xl, wl, cnt, loc = _lsort(route, u2, upper, lower)
        wsrc, tile_expert, nused, gw = _routing_tables(cnt[:, :, 0].astype(jnp.int32), nt)
        ys = _experts(wsrc, tile_expert, nused, xl, wl, w_up[l], w_down[l], nt)
        x = _combine(gw, ys, x1, mod_l, row_of_tile, loc.T, ln2_g[l].reshape(1, D), ln2_b[l].reshape(1, D))
        new_c.append(cn_c[..., 0:DH])
        new_n.append(cn_c[..., DH])
        new_m.append(jnp.swapaxes(m_c[..., 0], 1, 2))

    return (x[0:nc].reshape(bc, tc, D), x[nc:].reshape(bl, tl, D),
            jnp.stack(new_c, axis=1), jnp.stack(new_n, axis=1), jnp.stack(new_m, axis=1))
```

```python
import functools

import numpy as np
import jax
import jax.numpy as jnp
from jax import lax
from jax.experimental import pallas as pl
from jax.experimental.pallas import tpu as pltpu

F32 = jnp.float32
BF16 = jnp.bfloat16

D = 1024
DEPTH = 4
HEADS = 4
DH = 128
A_W = HEADS * DH
B_W = 256
B_GW = 64
C_W = 256
C_GW = 64
POOL_WINDOWS = (2, 4, 8, 16)
POOL_HALO = 8
N_GROUPS = 4
EXP_PER_GROUP = 4
N_EXPERTS = 16
D_EXPERT = 512
GRID_W = 64
POS_BASE = 10000.0
LN_EPS = 1e-5
ALPHA = (2 * DEPTH) ** 0.25
K_SCALE = DH ** -0.5
N_MOD_ROWS = 8

CH = 256
TM = 256
TE = 256
W_ROWS = 8
RL = 2 * TM + N_EXPERTS * W_ROWS
NSLOT = RL // W_ROWS
WPT = TE // W_ROWS
EP = 128
XW = D + 128
VMEM_LIMIT = 56 * 1024 * 1024


def _dot(a, b):
    return jnp.dot(a, b, preferred_element_type=F32)


def _dot_nt(a, b):
    return lax.dot_general(a, b, (((1,), (1,)), ((), ())), preferred_element_type=F32)


def _split2(x):
    hi = x.astype(BF16)
    lo = (x - hi.astype(F32)).astype(BF16)
    return hi, lo


def _split3(x):
    hi = x.astype(BF16)
    r = x - hi.astype(F32)
    mid = r.astype(BF16)
    lo = (r - mid.astype(F32)).astype(BF16)
    return hi, mid, lo


def _layer_norm(x, g, b):
    mu = jnp.mean(x, axis=-1, keepdims=True)
    xc = x - mu
    var = jnp.mean(xc * xc, axis=-1, keepdims=True)
    return xc * lax.rsqrt(var + LN_EPS) * g + b


def _sigmoid(x):
    return 1.0 / (1.0 + jnp.exp(-x))


def _log_sigmoid(x):
    return jnp.minimum(x, 0.0) - jnp.log1p(jnp.exp(-jnp.abs(x)))


def _params(sem):
    return pltpu.CompilerParams(dimension_semantics=sem, vmem_limit_bytes=VMEM_LIMIT)


def _ln_in_kernel(xp_ref, xs_ref, pos_ref, g_ref, b_ref, o_ref, *, n_ctx_tiles):
    i = pl.program_id(0)

    @pl.when(i < n_ctx_tiles)
    def _():
        o_ref[...] = _layer_norm(xp_ref[...], g_ref[...], b_ref[...])

    @pl.when(i >= n_ctx_tiles)
    def _():
        o_ref[...] = _layer_norm(xs_ref[...] + pos_ref[...], g_ref[...], b_ref[...])


def _ln_in(xp, xs, pos, g, b, t_lat):
    nc, ns = xp.shape[0], xs.shape[0]
    nct, nst, tps = nc // TM, ns // TM, t_lat // TM
    return pl.pallas_call(
        functools.partial(_ln_in_kernel, n_ctx_tiles=nct),
        grid=(nct + nst,),
        in_specs=[
            pl.BlockSpec((TM, D), lambda i: (jnp.minimum(i, nct - 1), 0)),
            pl.BlockSpec((TM, D), lambda i: (jnp.maximum(i - nct, 0), 0)),
            pl.BlockSpec((TM, D), lambda i: (jnp.maximum(i - nct, 0) % tps, 0)),
            pl.BlockSpec((1, D), lambda i: (0, 0)),
            pl.BlockSpec((1, D), lambda i: (0, 0)),
        ],
        out_specs=pl.BlockSpec((TM, D), lambda i: (i, 0)),
        out_shape=jax.ShapeDtypeStruct((nc + ns, D), F32),
        compiler_params=_params(("arbitrary",)),
        name="ln_in",
    )(xp, xs, pos, g, b)


def _mod_kernel(c_ref, w_ref, b_ref, o_ref):
    c = c_ref[...]
    s = c * _sigmoid(c)
    sh, sl = _split2(s)
    wh, wl = _split2(w_ref[...])
    o_ref[...] = _dot(sh, wh) + _dot(sl, wh) + _dot(sh, wl) + b_ref[...]


def _mod_all(cond, w_mod, b_mod):
    nj = 6 * D // 1024
    return pl.pallas_call(
        _mod_kernel,
        grid=(DEPTH, nj),
        in_specs=[
            pl.BlockSpec((N_MOD_ROWS, D), lambda l, j: (0, 0)),
            pl.BlockSpec((None, D, 1024), lambda l, j: (l, 0, j)),
            pl.BlockSpec((None, 1, 1024), lambda l, j: (l, 0, j)),
        ],
        out_specs=pl.BlockSpec((None, N_MOD_ROWS, 1024), lambda l, j: (l, 0, j)),
        out_shape=jax.ShapeDtypeStruct((DEPTH, N_MOD_ROWS, 6 * D), F32),
        compiler_params=_params(("arbitrary", "arbitrary")),
        name="mod",
    )(cond, w_mod, b_mod.reshape(DEPTH, 1, 6 * D))


def _inproj_kernel(x_ref, mod_ref, wqkv_ref, wkt_ref, wfp_ref, wifh_ref, wifl_ref, bif_ref,
                   qkv_ref, kt_ref, fp_ref, gi_ref, gf_ref):
    u = x_ref[...] * (1.0 + mod_ref[1:2, :]) + mod_ref[0:1, :]
    uh, ul = _split2(u)
    z = _dot(uh, wqkv_ref[...])
    qkv_ref[:, 0:A_W] = z[:, 0:A_W].astype(BF16)
    qkv_ref[:, A_W:2 * A_W] = (z[:, A_W:2 * A_W] * K_SCALE).astype(BF16)
    qkv_ref[:, 2 * A_W:3 * A_W] = z[:, 2 * A_W:3 * A_W].astype(BF16)
    kt_ref[...] = (_dot_nt(wkt_ref[...], uh) * K_SCALE).astype(BF16)
    fp_ref[...] = _dot(uh, wfp_ref[...])
    g = _dot(uh, wifh_ref[...]) + _dot(ul, wifh_ref[...]) + _dot(uh, wifl_ref[...]) + bif_ref[...]
    gi_ref[...] = g[:, 0:128]
    gf_ref[...] = g[:, 128:256]


def _inproj(x, mod_l, row_of_tile, wqkv, wkt, wfp, wifh, wifl, bif):
    n = x.shape[0]
    const = lambda i: (0, 0)
    return pl.pallas_call(
        _inproj_kernel,
        grid=(n // TM,),
        in_specs=[
            pl.BlockSpec((TM, D), lambda i: (i, 0)),
            pl.BlockSpec((None, 6, D), lambda i: (row_of_tile(i), 0, 0)),
            pl.BlockSpec((D, 3 * A_W), const),
            pl.BlockSpec((A_W, D), const),
            pl.BlockSpec((D, B_W + C_W), const),
            pl.BlockSpec((D, 256), const),
            pl.BlockSpec((D, 256), const),
            pl.BlockSpec((1, 256), const),
        ],
        out_specs=[
            pl.BlockSpec((TM, 3 * A_W), lambda i: (i, 0)),
            pl.BlockSpec((A_W, TM), lambda i: (0, i)),
            pl.BlockSpec((TM, B_W + C_W), lambda i: (i, 0)),
            pl.BlockSpec((TM, 128), lambda i: (i, 0)),
            pl.BlockSpec((TM, 128), lambda i: (i, 0)),
        ],
        out_shape=[
            jax.ShapeDtypeStruct((n, 3 * A_W), BF16),
            jax.ShapeDtypeStruct((A_W, n), BF16),
            jax.ShapeDtypeStruct((n, B_W + C_W), F32),
            jax.ShapeDtypeStruct((n, 128), F32),
            jax.ShapeDtypeStruct((n, 128), F32),
        ],
        compiler_params=_params(("arbitrary",)),
        name="inproj",
    )(x, mod_l, wqkv, wkt, wfp, wifh, wifl, bif)


def _gates_kernel(gi_ref, gf_ref, git_ref, gft_ref, tri_ref, trit_ref,
                  bcol_ref, rcol_ref, rrow_ref, gsum_ref, rmx_ref):
    tri = tri_ref[...]
    trit = trit_ref[...]
    h, m, l = _split3(_log_sigmoid(gf_ref[...]))
    cum_f = _dot(tri, h) + _dot(tri, m) + _dot(tri, l)
    cum_b = _dot(trit, h) + _dot(trit, m) + _dot(trit, l)
    lane = lax.broadcasted_iota(jnp.int32, (CH, 128), 1)
    bcol = jnp.where(lane < HEADS, cum_f, cum_b)
    bcol_ref[...] = bcol
    rcol_ref[...] = gi_ref[...] - bcol

    h, m, l = _split3(_log_sigmoid(gft_ref[...]))
    row_f = _dot(h, trit) + _dot(m, trit) + _dot(l, trit)
    row_b = _dot(h, tri) + _dot(m, tri) + _dot(l, tri)
    row = lax.broadcasted_iota(jnp.int32, (8, CH), 0)
    brow = jnp.where(row < HEADS, row_f, row_b)
    rrow = git_ref[...] - brow
    rrow_ref[...] = rrow
    row1 = lax.broadcasted_iota(jnp.int32, (8, 1), 0)
    total = jnp.where(row1 < HEADS, brow[:, CH - 1:CH], brow[:, 0:1])
    gsum_ref[...] = jnp.broadcast_to(total, (8, 128))
    rmx_ref[...] = jnp.broadcast_to(jnp.max(rrow, axis=1, keepdims=True), (8, 128))


def _gates(gi, gf, git, gft, tri, trit):
    n = gi.shape[0]
    nch = n // CH
    const = lambda i: (0, 0)
    return pl.pallas_call(
        _gates_kernel,
        grid=(nch,),
        in_specs=[
            pl.BlockSpec((CH, 128), lambda i: (i, 0)),
            pl.BlockSpec((CH, 128), lambda i: (i, 0)),
            pl.BlockSpec((8, CH), lambda i: (0, i)),
            pl.BlockSpec((8, CH), lambda i: (0, i)),
            pl.BlockSpec((CH, CH), const),
            pl.BlockSpec((CH, CH), const),
        ],
        out_specs=[
            pl.BlockSpec((CH, 128), lambda i: (i, 0)),
            pl.BlockSpec((CH, 128), lambda i: (i, 0)),
            pl.BlockSpec((8, CH), lambda i: (0, i)),
            pl.BlockSpec((None, 8, 128), lambda i: (i, 0, 0)),
            pl.BlockSpec((None, 8, 128), lambda i: (i, 0, 0)),
        ],
        out_shape=[
            jax.ShapeDtypeStruct((n, 128), F32),
            jax.ShapeDtypeStruct((n, 128), F32),
            jax.ShapeDtypeStruct((8, n), F32),
            jax.ShapeDtypeStruct((nch, 8, 128), F32),
            jax.ShapeDtypeStruct((nch, 8, 128), F32),
        ],
        compiler_params=_params(("arbitrary",)),
        name="gates",
    )(gi, gf, git, gft, tri, trit)


def _mlstm_kernel(*refs, nchunk, has_state):
    if has_state:
        (q_ref, k_ref, v_ref, kt_ref, bcol_ref, rcol_ref, rrow_ref, gsum_ref, rmx_ref, lng_ref,
         cn0_ref, m0_ref, hn_ref, cn_st) = refs
    else:
        (q_ref, k_ref, v_ref, kt_ref, bcol_ref, rcol_ref, rrow_ref, gsum_ref, rmx_ref, lng_ref,
         hn_ref, cn_out_ref, m_out_ref) = refs
    head = pl.program_id(1)
    lane = lax.broadcasted_iota(jnp.int32, (CH, 128), 1)
    tpos = lax.broadcasted_iota(jnp.int32, (CH, CH), 0)
    spos = lax.broadcasted_iota(jnp.int32, (CH, CH), 1)
    masks = (spos <= tpos, spos >= tpos)

    def col(ref, j, idx):
        blk = ref[j * CH:(j + 1) * CH, :]
        return jnp.sum(jnp.where(lane == idx, blk, 0.0), axis=1, keepdims=True)

    m_in = [[None] * nchunk, [None] * nchunk]
    for d in range(2):
        idx = d * HEADS + head
        cn = cn0_ref[d] if has_state else None
        m = m0_ref[pl.ds(idx, 1), 0:1] if has_state else None
        order = range(nchunk) if d == 0 else range(nchunk - 1, -1, -1)
        for pos_in_sweep, j in enumerate(order):
            if m is not None:
                cn_st[d, j] = cn
                m_in[d][j] = m
            if has_state and pos_in_sweep == nchunk - 1:
                break
            total = gsum_ref[j, pl.ds(idx, 1), 0:1]
            rmax = rmx_ref[j, pl.ds(idx, 1), 0:1]
            mm = rmax if m is None else jnp.maximum(m, rmax)
            ws = jnp.exp(col(rcol_ref, j, idx) - mm)
            vj = v_ref[j * CH:(j + 1) * CH, :].astype(F32)
            vs = jnp.concatenate([(vj * ws).astype(BF16),
                                  jnp.where(lane == 0, ws, 0.0).astype(BF16)], axis=1)
            upd = _dot(kt_ref[:, j * CH:(j + 1) * CH], vs)
            cn = upd if m is None else jnp.exp(m - mm) * cn + upd
            m = total + mm
        if not has_state:
            cn_out_ref[d] = cn
            m_out_ref[d:d + 1, :] = jnp.broadcast_to(m, (1, 128))

    for j in range(nchunk):
        sl = slice(j * CH, (j + 1) * CH)
        qj = q_ref[sl, :]
        s_raw = _dot_nt(qj, k_ref[sl, :])
        p_sum, h_inter = None, None
        for d in range(2):
            idx = d * HEADS + head
            rr = rrow_ref[pl.ds(idx, 1), sl]
            m_row = jnp.max(jnp.where(masks[d], rr, -jnp.inf), axis=1, keepdims=True)
            m_prev = m_in[d][j]
            if m_prev is not None:
                m_row = jnp.maximum(m_row, m_prev)
            p = s_raw * jnp.where(masks[d], jnp.exp(rr - m_row), 0.0)
            den = jnp.sum(p, axis=1, keepdims=True)
            if m_prev is not None:
                inter = jnp.exp(m_prev - m_row)
                qc = _dot(qj, cn_st[d, j].astype(BF16))
                den = den + inter * qc[:, DH:DH + 1]
            floor = jnp.exp(-(col(bcol_ref, j, idx) + m_row))
            rinv = 1.0 / jnp.maximum(jnp.abs(den), floor)
            p_sum = p * rinv if p_sum is None else p_sum + p * rinv
            if m_prev is not None:
                hi = (inter * rinv) * qc[:, 0:DH]
                h_inter = hi if h_inter is None else h_inter + hi
        hj = _dot(p_sum.astype(BF16), v_ref[sl, :])
        if h_inter is not None:
            hj = hj + h_inter
        mu = jnp.mean(hj, axis=-1, keepdims=True)
        hc = hj - mu
        var = jnp.mean(hc * hc, axis=-1, keepdims=True)
        hn_ref[sl, :] = hc * lax.rsqrt(var + LN_EPS) * lng_ref[...]


def _mlstm(qkv, kt, bcol, rcol, rrow, gsum, rmx, lng, tok0, nseq, t, cn0=None, m0=None):
    nchunk = t // CH
    sb = tok0 // t
    cb = tok0 // CH
    has_state = cn0 is not None
    in_specs = [
        pl.BlockSpec((t, DH), lambda s, h: (sb + s, h)),
        pl.BlockSpec((t, DH), lambda s, h: (sb + s, HEADS + h)),
        pl.BlockSpec((t, DH), lambda s, h: (sb + s, 2 * HEADS + h)),
        pl.BlockSpec((DH, t), lambda s, h: (h, sb + s)),
        pl.BlockSpec((t, 128), lambda s, h: (sb + s, 0)),
        pl.BlockSpec((t, 128), lambda s, h: (sb + s, 0)),
        pl.BlockSpec((8, t), lambda s, h: (0, sb + s)),
        pl.BlockSpec((nchunk, 8, 128), lambda s, h: (cb // nchunk + s, 0, 0)),
        pl.BlockSpec((nchunk, 8, 128), lambda s, h: (cb // nchunk + s, 0, 0)),
        pl.BlockSpec((1, DH), lambda s, h: (0, h)),
    ]
    args = [qkv, qkv, qkv, kt, bcol, rcol, rrow, gsum, rmx, lng]
    hn_spec = pl.BlockSpec((t, DH), lambda s, h: (s, h))
    hn_shape = jax.ShapeDtypeStruct((nseq * t, A_W), F32)
    if has_state:
        in_specs += [
            pl.BlockSpec((None, 2, None, DH, 2 * DH), lambda s, h: (s, 0, h, 0, 0)),
            pl.BlockSpec((None, 8, 128), lambda s, h: (s, 0, 0)),
        ]
        args += [cn0, m0]
        out_specs, out_shape = hn_spec, hn_shape
        scratch = [pltpu.VMEM((2, nchunk, DH, 2 * DH), F32)]
    else:
        out_specs = [
            hn_spec,
            pl.BlockSpec((None, 2, None, DH, 2 * DH), lambda s, h: (s, 0, h, 0, 0)),
            pl.BlockSpec((None, None, 2, 128), lambda s, h: (s, h, 0, 0)),
        ]
        out_shape = [
            hn_shape,
            jax.ShapeDtypeStruct((nseq, 2, HEADS, DH, 2 * DH), F32),
            jax.ShapeDtypeStruct((nseq, HEADS, 2, 128), F32),
        ]
        scratch = []
    return pl.pallas_call(
        functools.partial(_mlstm_kernel, nchunk=nchunk, has_state=has_state),
        grid=(nseq, HEADS),
        in_specs=in_specs,
        out_specs=out_specs,
        out_shape=out_shape,
        scratch_shapes=scratch,
        compiler_params=_params(("arbitrary", "arbitrary")),
        name="mlstm_lat" if has_state else "mlstm_ctx",
    )(*args)


def _fftpool_kernel(fp_ref, cth_ref, ctl_ref, sth_ref, stl_ref, bcsh_ref, bcsl_ref, icnt_ref,
                    wpool_ref, pscale_ref, yf_ref, pp_ref, pad_ref, *, t):
    fh, fl = _split2(fp_ref[:, 0:B_W])
    xcs = _dot(fh, bcsh_ref[...]) + _dot(fl, bcsh_ref[...]) + _dot(fh, bcsl_ref[...])
    xch, xcl = _split2(xcs[:, 0:B_W])
    xsh, xsl = _split2(xcs[:, B_W:2 * B_W])
    y = (_dot(cth_ref[...], xch) + _dot(ctl_ref[...], xch) + _dot(cth_ref[...], xcl)
         - _dot(sth_ref[...], xsh) - _dot(stl_ref[...], xsh) - _dot(sth_ref[...], xsl))
    yf_ref[...] = (y * ((t * B_GW) ** -0.5)).astype(BF16)

    p = fp_ref[:, B_W:B_W + C_W]
    zero = jnp.zeros((POOL_HALO, C_W), F32)
    pad_ref[0:POOL_HALO, :] = zero
    pad_ref[POOL_HALO + t:2 * POOL_HALO + t, :] = zero
    pad_ref[POOL_HALO:POOL_HALO + t, :] = p

    def sh(dlt):
        return pad_ref[POOL_HALO + dlt:POOL_HALO + dlt + t, :]

    w2 = sh(-1) + p
    w4 = w2 + sh(-2) + sh(1)
    w8 = w4 + sh(-4) + sh(-3) + sh(2) + sh(3)
    w16 = w8 + sh(-8) + sh(-7) + sh(-6) + sh(-5) + sh(4) + sh(5) + sh(6) + sh(7)
    grp = lax.broadcasted_iota(jnp.int32, (t, C_W), 1) // C_GW
    wsum = jnp.where(grp == 0, w2, jnp.where(grp == 1, w4, jnp.where(grp == 2, w8, w16)))
    pooled = wsum * icnt_ref[...] - p
    pp_ref[...] = (_dot(pooled.astype(BF16), wpool_ref[...]) * pscale_ref[...]).astype(BF16)


def _fftpool(fp, consts, wpool_bd, pscale, tok0, nseq, t):
    cth, ctl, sth, stl, bcsh, bcsl, icnt = consts
    sb = tok0 // t
    const = lambda s: (0, 0)
    return pl.pallas_call(
        functools.partial(_fftpool_kernel, t=t),
        grid=(nseq,),
        in_specs=[
            pl.BlockSpec((t, B_W + C_W), lambda s: (sb + s, 0)),
            pl.BlockSpec((t, t), const), pl.BlockSpec((t, t), const),
            pl.BlockSpec((t, t), const), pl.BlockSpec((t, t), const),
            pl.BlockSpec((B_W, 2 * B_W), const), pl.BlockSpec((B_W, 2 * B_W), const),
            pl.BlockSpec((t, C_W), const),
            pl.BlockSpec((C_W, C_W), const),
            pl.BlockSpec((1, C_W), const),
        ],
        out_specs=[pl.BlockSpec((t, B_W), lambda s: (s, 0)), pl.BlockSpec((t, C_W), lambda s: (s, 0))],
        out_shape=[jax.ShapeDtypeStruct((nseq * t, B_W), BF16), jax.ShapeDtypeStruct((nseq * t, C_W), BF16)],
        scratch_shapes=[pltpu.VMEM((t + 2 * POOL_HALO, C_W), F32)],
        compiler_params=_params(("arbitrary",)),
        name=f"fftpool_{t}",
    )(fp, cth, ctl, sth, stl, bcsh, bcsl, icnt, wpool_bd, pscale)


def _merge_kernel(x_ref, mod_ref, hn_ref, yf_ref, pp_ref, wo_ref, wg_ref, wpa_ref, wpb_ref, wpc_ref,
                  wout_ref, g1_ref, b1_ref, wrh_ref, wrl_ref, br_ref, x1_ref, u2_ref, route_ref):
    x = x_ref[...]
    ub = (x * (1.0 + mod_ref[1:2, :]) + mod_ref[0:1, :]).astype(BF16)
    a = (hn_ref[...] * _sigmoid(_dot(ub, wo_ref[...]))).astype(BF16)
    acc = _sigmoid(_dot(ub, wg_ref[:, 0:D])) * _dot(a, wpa_ref[...])
    acc = acc + _sigmoid(_dot(ub, wg_ref[:, D:2 * D])) * _dot(yf_ref[...], wpb_ref[...])
    acc = acc + _sigmoid(_dot(ub, wg_ref[:, 2 * D:3 * D])) * _dot(pp_ref[...], wpc_ref[...])
    out = _dot(acc.astype(BF16), wout_ref[...])
    x1 = _layer_norm(ALPHA * x + mod_ref[2:3, :] * out, g1_ref[...], b1_ref[...])
    x1_ref[...] = x1
    u2 = x1 * (1.0 + mod_ref[4:5, :]) + mod_ref[3:4, :]
    u2_ref[...] = u2.astype(BF16)

    uh, ul = _split2(u2)
    lt = _dot_nt(wrh_ref[...], uh) + _dot_nt(wrh_ref[...], ul) + _dot_nt(wrl_ref[...], uh) + br_ref[:, 0:1]
    gl = [lt[g:g + 1, :] for g in range(N_GROUPS)]
    gmax = jnp.maximum(jnp.maximum(gl[0], gl[1]), jnp.maximum(gl[2], gl[3]))
    gidx = jnp.where(gl[0] >= gmax, 0, jnp.where(gl[1] >= gmax, 1, jnp.where(gl[2] >= gmax, 2, 3)))
    gval = 1.0 / (jnp.exp(gl[0] - gmax) + jnp.exp(gl[1] - gmax) + jnp.exp(gl[2] - gmax) + jnp.exp(gl[3] - gmax))
    fe = []
    for e in range(EXP_PER_GROUP):
        r = [lt[N_GROUPS + g * EXP_PER_GROUP + e:N_GROUPS + g * EXP_PER_GROUP + e + 1, :] for g in range(N_GROUPS)]
        fe.append(jnp.where(gidx == 0, r[0], jnp.where(gidx == 1, r[1], jnp.where(gidx == 2, r[2], r[3]))))

    def top1(v):
        vmax = jnp.maximum(jnp.maximum(v[0], v[1]), jnp.maximum(v[2], v[3]))
        imax = jnp.where(v[0] >= vmax, 0, jnp.where(v[1] >= vmax, 1, jnp.where(v[2] >= vmax, 2, 3)))
        return vmax, imax

    v1, i1 = top1(fe)
    v2, i2 = top1([jnp.where(i1 == e, -jnp.inf, fe[e]) for e in range(EXP_PER_GROUP)])
    tt = jnp.exp(v2 - v1)
    w1 = gval / (1.0 + tt)
    w2 = gval * tt / (1.0 + tt)
    e1 = (gidx * EXP_PER_GROUP + i1).astype(F32)
    e2 = (gidx * EXP_PER_GROUP + i2).astype(F32)
    route_ref[...] = jnp.zeros(route_ref.shape, F32)
    route_ref[0:1, :] = e1
    route_ref[1:2, :] = e2
    route_ref[2:3, :] = w1
    route_ref[3:4, :] = w2


def _merge(x, mod_l, row_of_tile, hn, yf, pp, wo, wg, wpa, wpb, wpc, wout, g1, b1, wrh, wrl, br):
    n = x.shape[0]
    const = lambda i: (0, 0)
    tok = lambda i: (i, 0)
    return pl.pallas_call(
        _merge_kernel,
        grid=(n // TM,),
        in_specs=[
            pl.BlockSpec((TM, D), tok),
            pl.BlockSpec((None, 6, D), lambda i: (row_of_tile(i), 0, 0)),
            pl.BlockSpec((TM, A_W), tok),
            pl.BlockSpec((TM, B_W), tok),
            pl.BlockSpec((TM, C_W), tok),
            pl.BlockSpec((D, A_W), const),
            pl.BlockSpec((D, 3 * D), const),
            pl.BlockSpec((A_W, D), const),
            pl.BlockSpec((B_W, D), const),
            pl.BlockSpec((C_W, D), const),
            pl.BlockSpec((D, D), const),
            pl.BlockSpec((1, D), const),
            pl.BlockSpec((1, D), const),
            pl.BlockSpec((32, D), const),
            pl.BlockSpec((32, D), const),
            pl.BlockSpec((32, 128), const),
        ],
        out_specs=[pl.BlockSpec((TM, D), tok), pl.BlockSpec((TM, D), tok), pl.BlockSpec((8, TM), lambda i: (0, i))],
        out_shape=[jax.ShapeDtypeStruct((n, D), F32), jax.ShapeDtypeStruct((n, D), BF16),
                   jax.ShapeDtypeStruct((8, n), F32)],
        compiler_params=_params(("arbitrary",)),
        name="merge",
    )(x, mod_l, hn, yf, pp, wo, wg, wpa, wpb, wpc, wout, g1, b1, wrh, wrl, br)


def _lsort_kernel(route_ref, u2_ref, upper_ref, lower_ref, xl_ref, cnt_ref, loc_ref):
    e0 = route_ref[0:1, :].astype(jnp.int32)
    e1 = route_ref[1:2, :].astype(jnp.int32)
    eid = lax.broadcasted_iota(jnp.int32, (EP, TM), 0)
    m0 = eid == e0
    m1 = eid == e1
    member = jnp.where(m0 | m1, 1.0, 0.0)
    rank = _dot(member.astype(BF16), upper_ref[...])
    cnt = jnp.sum(member, axis=1, keepdims=True)
    nwin = jnp.floor((cnt + (W_ROWS - 1)) * (1.0 / W_ROWS))
    first = _dot(lower_ref[...], jnp.broadcast_to(nwin, (EP, 128)).astype(BF16))[:, 0:1] * W_ROWS
    r0 = jnp.sum(jnp.where(m0, first + rank, 0.0), axis=0, keepdims=True).astype(jnp.int32)
    r1 = jnp.sum(jnp.where(m1, first + rank, 0.0), axis=0, keepdims=True).astype(jnp.int32)
    rid = lax.broadcasted_iota(jnp.int32, (RL, TM), 0)
    h0 = rid == r0
    h1 = rid == r1
    xl_ref[:, 0:D] = _dot(jnp.where(h0 | h1, 1.0, 0.0).astype(BF16), u2_ref[...])
    wsel = jnp.where(h0, route_ref[2:3, :], 0.0) + jnp.where(h1, route_ref[3:4, :], 0.0)
    xl_ref[:, D:XW] = jnp.broadcast_to(jnp.sum(wsel, axis=1, keepdims=True), (RL, XW - D))
    cnt_ref[...] = jnp.broadcast_to(cnt[0:N_EXPERTS, :], (N_EXPERTS, 128))
    loc_ref[...] = jnp.zeros(loc_ref.shape, F32)
    loc_ref[0:1, :] = r0.astype(F32)
    loc_ref[1:2, :] = r1.astype(F32)


def _lsort(route, u2, upper, lower):
    n = u2.shape[0]
    nb = n // TM
    const = lambda i: (0, 0)
    return pl.pallas_call(
        _lsort_kernel,
        grid=(nb,),
        in_specs=[
            pl.BlockSpec((8, TM), lambda i: (0, i)),
            pl.BlockSpec((TM, D), lambda i: (i, 0)),
            pl.BlockSpec((TM, TM), const),
            pl.BlockSpec((EP, EP), const),
        ],
        out_specs=[
            pl.BlockSpec((RL, XW), lambda i: (i, 0)),
            pl.BlockSpec((None, N_EXPERTS, 128), lambda i: (i, 0, 0)),
            pl.BlockSpec((8, TM), lambda i: (0, i)),
        ],
        out_shape=[
            jax.ShapeDtypeStruct((nb * RL, XW), F32),
            jax.ShapeDtypeStruct((nb, N_EXPERTS, 128), F32),
            jax.ShapeDtypeStruct((8, n), F32),
        ],
        compiler_params=_params(("arbitrary",)),
        name="lsort",
    )(route, u2, upper, lower)


def _fetch_windows(src_hbm, rows_ref, first, n_windows, dst, sem):
    def body(w, c):
        src = pl.ds(pl.multiple_of(rows_ref[first + w], W_ROWS), W_ROWS)
        pltpu.make_async_copy(src_hbm.at[src], dst.at[pl.ds(pl.multiple_of(w * W_ROWS, W_ROWS), W_ROWS)], sem).start()
        return c

    lax.fori_loop(0, n_windows, body, 0, unroll=8)


def _wait_windows(src_hbm, dst, sem):
    pltpu.make_async_copy(src_hbm.at[pl.ds(0, dst.shape[0])], dst, sem).wait()


def _experts_kernel(wsrc_ref, te_ref, nused_ref, xl_hbm, wup_ref, wdn_ref, ys_ref, xbuf, sems):
    j = pl.program_id(0)
    nused = nused_ref[0]
    slot = j % 2

    @pl.when(j == 0)
    def _():
        _fetch_windows(xl_hbm, wsrc_ref, 0, WPT, xbuf.at[0], sems.at[0])

    @pl.when(j + 1 < nused)
    def _():
        _fetch_windows(xl_hbm, wsrc_ref, (j + 1) * WPT, WPT, xbuf.at[1 - slot], sems.at[1 - slot])

    @pl.when(j < nused)
    def _():
        _wait_windows(xl_hbm, xbuf.at[slot], sems.at[slot])
        xw = xbuf[slot]
        hid = _dot(xw[:, 0:D].astype(BF16), wup_ref[...].astype(BF16))
        a = hid[:, 0:D_EXPERT]
        act = a * _sigmoid(a) * hid[:, D_EXPERT:2 * D_EXPERT] * xw[:, D:D + 1]
        ys_ref[...] = _dot(act.astype(BF16), wdn_ref[...].astype(BF16))

    @pl.when(j >= nused)
    def _():
        ys_ref[...] = jnp.zeros(ys_ref.shape, F32)


def _experts(wsrc, tile_expert, nused, xl, w_up_l, w_down_l, nt):
    return pl.pallas_call(
        _experts_kernel,
        grid_spec=pltpu.PrefetchScalarGridSpec(
            num_scalar_prefetch=3,
            grid=(nt,),
            in_specs=[
                pl.BlockSpec(memory_space=pl.ANY),
                pl.BlockSpec((None, D, 2 * D_EXPERT), lambda i, ws, te, nu: (te[i], 0, 0)),
                pl.BlockSpec((None, D_EXPERT, D), lambda i, ws, te, nu: (te[i], 0, 0)),
            ],
            out_specs=pl.BlockSpec((TE, D), lambda i, ws, te, nu: (i, 0)),
            scratch_shapes=[pltpu.VMEM((2, TE, XW), F32), pltpu.SemaphoreType.DMA((2,))],
        ),
        out_shape=jax.ShapeDtypeStruct((nt * TE, D), F32),
        compiler_params=_params(("arbitrary",)),
        name="experts",
    )(wsrc, tile_expert, nused, xl, w_up_l, w_down_l)


def _combine_kernel(gw_ref, ys_hbm, x1_ref, mod_ref, loc_ref, g2_ref, b2_ref, o_ref, ybuf, sems):
    i = pl.program_id(0)
    slot = i % 2

    @pl.when(i == 0)
    def _():
        _fetch_windows(ys_hbm, gw_ref, 0, NSLOT, ybuf.at[0], sems.at[0])

    @pl.when(i + 1 < pl.num_programs(0))
    def _():
        _fetch_windows(ys_hbm, gw_ref, (i + 1) * NSLOT, NSLOT, ybuf.at[1 - slot], sems.at[1 - slot])

    _wait_windows(ys_hbm, ybuf.at[slot], sems.at[slot])
    r0 = loc_ref[:, 0:1].astype(jnp.int32)
    r1 = loc_ref[:, 1:2].astype(jnp.int32)
    lid = lax.broadcasted_iota(jnp.int32, (TM, RL), 1)
    pt = jnp.where((lid == r0) | (lid == r1), 1.0, 0.0).astype(BF16)
    yh, yl = _split2(ybuf[slot])
    y = _dot(pt, yh) + _dot(pt, yl)
    o_ref[...] = _layer_norm(ALPHA * x1_ref[...] + mod_ref[5:6, :] * y, g2_ref[...], b2_ref[...])


def _combine(gw, ys, x1, mod_l, row_of_tile, loc_t, g2, b2):
    n = x1.shape[0]
    return pl.pallas_call(
        _combine_kernel,
        grid_spec=pltpu.PrefetchScalarGridSpec(
            num_scalar_prefetch=1,
            grid=(n // TM,),
            in_specs=[
                pl.BlockSpec(memory_space=pl.ANY),
                pl.BlockSpec((TM, D), lambda i, gw: (i, 0)),
                pl.BlockSpec((None, 6, D), lambda i, gw: (row_of_tile(i), 0, 0)),
                pl.BlockSpec((TM, 8), lambda i, gw: (i, 0)),
                pl.BlockSpec((1, D), lambda i, gw: (0, 0)),
                pl.BlockSpec((1, D), lambda i, gw: (0, 0)),
            ],
            out_specs=pl.BlockSpec((TM, D), lambda i, gw: (i, 0)),
            scratch_shapes=[pltpu.VMEM((2, RL, D), F32), pltpu.SemaphoreType.DMA((2,))],
        ),
        out_shape=jax.ShapeDtypeStruct((n, D), F32),
        compiler_params=_params(("arbitrary",)),
        name="combine",
    )(gw, ys, x1, mod_l, loc_t, g2, b2)


def _routing_tables(cnt, nt):
    nb = cnt.shape[0]
    nwin = (cnt + W_ROWS - 1) // W_ROWS
    lo_inc = jnp.cumsum(nwin, axis=1)
    lo = lo_inc - nwin
    bo_inc = jnp.cumsum(nwin, axis=0)
    bo = bo_inc - nwin
    we = bo_inc[-1]
    wpad = ((we + WPT - 1) // WPT) * WPT
    eend = jnp.cumsum(wpad)
    eoff = eend - wpad
    nused = (eend[-1] // WPT).reshape(1)
    tile_expert = jnp.minimum(
        jnp.sum((jnp.arange(nt, dtype=jnp.int32)[:, None] * WPT >= eend[None, :]).astype(jnp.int32), axis=1),
        N_EXPERTS - 1)
    experts = jnp.arange(N_EXPERTS, dtype=jnp.int32)
    pick = lambda onehot, table: jnp.sum(jnp.where(onehot, table, 0), axis=-1)
    g = jnp.arange(nt * WPT, dtype=jnp.int32)
    oh_e = jnp.repeat(tile_expert[:, None] == experts[None, :], WPT, axis=0)
    ig = g - pick(oh_e, eoff[None, :])
    binc_g = pick(oh_e[:, None, :], bo_inc[None, :, :])
    bg = jnp.minimum(jnp.sum((binc_g <= ig[:, None]).astype(jnp.int32), axis=1), nb - 1)
    oh_b = bg[:, None] == jnp.arange(nb, dtype=jnp.int32)[None, :]
    shift_g = pick(oh_b, pick(oh_e[:, None, :], (lo - bo)[None, :, :]))
    empty_window_row = RL - W_ROWS
    wsrc = jnp.where(ig < pick(oh_e, we[None, :]), bg * RL + (shift_g + ig) * W_ROWS, empty_window_row)
    s = jnp.arange(NSLOT, dtype=jnp.int32)
    es = jnp.minimum(jnp.sum((lo_inc[:, None, :] <= s[None, :, None]).astype(jnp.int32), axis=2), N_EXPERTS - 1)
    oh_s = es[:, :, None] == experts[None, None, :]
    gwin = pick(oh_s, (eoff[None, :] + bo - lo)[:, None, :]) + s[None, :]
    gw = jnp.where(s[None, :] < lo_inc[:, -1:], gwin, 0) * W_ROWS
    i32 = lambda a: a.astype(jnp.int32)
    return i32(wsrc), i32(tile_expert), i32(nused), i32(gw).reshape(-1)


def _grid_pos_embed(n_tok):
    rows = n_tok // GRID_W
    r, col = np.meshgrid(np.arange(rows, dtype=np.float32), np.arange(GRID_W, dtype=np.float32), indexing="ij")
    r, col = r.reshape(-1), col.reshape(-1)
    quarter = D // 4
    freq = (1.0 / (np.float32(POS_BASE) ** (np.arange(quarter, dtype=np.float32) / np.float32(quarter)))).astype(np.float32)
    ang_r = r[:, None] * freq[None, :]
    ang_c = col[:, None] * freq[None, :]
    return jnp.asarray(np.concatenate([np.sin(ang_r), np.cos(ang_r), np.sin(ang_c), np.cos(ang_c)], axis=-1),
                       dtype=F32)


def _dft_consts(t):
    def hl(a):
        hi = a.astype(BF16)
        lo = (a - hi.astype(np.float64)).astype(BF16)
        return jnp.asarray(hi), jnp.asarray(lo)

    tt = np.arange(t)
    ang = 2.0 * np.pi * ((tt[:, None] * tt[None, :]) % t) / t
    cth, ctl = hl(np.cos(ang))
    sth, stl = hl(np.sin(ang))
    cc = np.arange(B_GW)
    angc = 2.0 * np.pi * ((cc[:, None] * cc[None, :]) % B_GW) / B_GW
    eye = np.eye(B_W // B_GW)
    bcs = np.concatenate([np.kron(eye, np.cos(angc)), np.kron(eye, np.sin(angc))], axis=1)
    bcsh, bcsl = hl(bcs)
    pos = np.arange(t)
    icnt = np.concatenate(
        [np.repeat((1.0 / (np.minimum(pos + w // 2, t) - np.maximum(pos - w // 2, 0)))[:, None], C_GW, axis=1)
         for w in POOL_WINDOWS], axis=1)
    return cth, ctl, sth, stl, bcsh, bcsl, jnp.asarray(icnt, F32)


def kernel(x_prompt, x_sample, c, state_C, state_n, state_m, c_ctx, ln_in_g, ln_in_b, w_mod, b_mod, w_in, b_if, ln_a_g, w_pa, w_pb, w_pc, w_pool, pool_scale, w_out, ln1_g, ln1_b, w_r1, b_r1, w_r2, b_r2, w_up, w_down, ln2_g, ln2_b):
    bc, tc, _ = x_prompt.shape
    bl, tl, _ = x_sample.shape
    nc, nl = bc * tc, bl * tl
    n = nc + nl
    assert tc == CH and tl % CH == 0 and nc % tl == 0 and bl + 1 <= N_MOD_ROWS
    nt = -(-(2 * n // W_ROWS + (n // TM) * N_EXPERTS + N_EXPERTS * (WPT - 1)) // WPT)
    upper = jnp.asarray(np.triu(np.ones((TM, TM), np.float32), 1), BF16)
    lower = jnp.asarray(np.tril(np.ones((EP, EP), np.float32), -1), BF16)

    def row_of_tile(i):
        return jnp.where(i < nc // TM, 0, 1 + jnp.maximum(i - nc // TM, 0) // (tl // TM))

    wqkv = w_in[:, :, 0:3 * A_W].astype(BF16)
    wkt = jnp.swapaxes(w_in[:, :, A_W:2 * A_W], 1, 2).astype(BF16)
    wo = w_in[:, :, 3 * A_W:4 * A_W].astype(BF16)
    c0 = 4 * A_W
    wif = w_in[:, :, c0:c0 + 16].reshape(DEPTH, D, 2, 2, HEADS)
    zpad = jnp.zeros((DEPTH, D, 128 - 2 * HEADS), F32)
    wif = jnp.concatenate([wif[:, :, :, 0, :].reshape(DEPTH, D, 2 * HEADS), zpad,
                           wif[:, :, :, 1, :].reshape(DEPTH, D, 2 * HEADS), zpad], axis=2)
    wifh, wifl = _split2(wif)
    bpad = jnp.zeros((DEPTH, 128 - 2 * HEADS), F32)
    bif = jnp.concatenate([b_if[:, :, 0, :].reshape(DEPTH, 2 * HEADS), bpad,
                           b_if[:, :, 1, :].reshape(DEPTH, 2 * HEADS), bpad], axis=1).reshape(DEPTH, 1, 256)
    wfp = w_in[:, :, c0 + 16:c0 + 16 + B_W + C_W].astype(BF16)
    wg = w_in[:, :, c0 + 16 + B_W + C_W:].astype(BF16)
    wpa, wpb, wpc, wout = (w.astype(BF16) for w in (w_pa, w_pb, w_pc, w_out))
    eye = jnp.eye(C_W // C_GW, dtype=F32)
    wpool_bd = jnp.einsum("gh,lgce->lgche", eye, w_pool).reshape(DEPTH, C_W, C_W).astype(BF16)
    wr = jnp.concatenate([w_r1, w_r2.reshape(DEPTH, D, N_EXPERTS),
                          jnp.zeros((DEPTH, D, 32 - N_GROUPS - N_EXPERTS), F32)], axis=2)
    wrh, wrl = _split2(jnp.swapaxes(wr, 1, 2))
    br = jnp.concatenate([b_r1, b_r2.reshape(DEPTH, N_EXPERTS),
                          jnp.zeros((DEPTH, 32 - N_GROUPS - N_EXPERTS), F32)], axis=1)
    br = jnp.broadcast_to(br[:, :, None], (DEPTH, 32, 128))

    tri_np = np.tril(np.ones((CH, CH), np.float32))
    tri, trit = jnp.asarray(tri_np, BF16), jnp.asarray(tri_np.T, BF16)
    consts_c, consts_l = _dft_consts(tc), _dft_consts(tl)

    cond = jnp.concatenate([c_ctx[None, :], c, jnp.zeros((N_MOD_ROWS - 1 - bl, D), F32)], axis=0)
    mod = _mod_all(cond, w_mod, b_mod).reshape(DEPTH, N_MOD_ROWS, 6, D)

    x = _ln_in(x_prompt.reshape(nc, D), x_sample.reshape(nl, D), _grid_pos_embed(tl),
               ln_in_g.reshape(1, D), ln_in_b.reshape(1, D), tl)

    cn0 = jnp.concatenate([state_C, state_n[..., None], jnp.zeros(state_C.shape[:-1] + (DH - 1,), F32)], axis=-1)
    m0 = jnp.broadcast_to(state_m.reshape(bl, DEPTH, 2 * HEADS, 1), (bl, DEPTH, 2 * HEADS, 128))

    new_c, new_n, new_m = [], [], []
    for l in range(DEPTH):
        mod_l = mod[l]
        qkv, kt, fp, gi, gf = _inproj(x, mod_l, row_of_tile, wqkv[l], wkt[l], wfp[l], wifh[l], wifl[l], bif[l])
        bcol, rcol, rrow, gsum, rmx = _gates(gi, gf, gi[:, 0:8].T, gf[:, 0:8].T, tri, trit)
        lng = ln_a_g[l].reshape(1, A_W)
        hn_c, cn_c, m_c = _mlstm(qkv, kt, bcol, rcol, rrow, gsum, rmx, lng, 0, bc, tc)
        hn_l = _mlstm(qkv, kt, bcol, rcol, rrow, gsum, rmx, lng, nc, bl, tl, cn0[:, l], m0[:, l])
        pscale = pool_scale[l].reshape(1, C_W)
        yf_c, pp_c = _fftpool(fp, consts_c, wpool_bd[l], pscale, 0, bc, tc)
        yf_l, pp_l = _fftpool(fp, consts_l, wpool_bd[l], pscale, nc, bl, tl)
        hn = jnp.concatenate([hn_c, hn_l], axis=0)
        yf = jnp.concatenate([yf_c, yf_l], axis=0)
        pp = jnp.concatenate([pp_c, pp_l], axis=0)
        x1, u2, route = _merge(x, mod_l, row_of_tile, hn, yf, pp, wo[l], wg[l], wpa[l], wpb[l], wpc[l], wout[l],
                               ln1_g[l].reshape(1, D), ln1_b[l].reshape(1, D), wrh[l], wrl[l], br[l])
        xl, cnt, loc = _lsort(route, u2, upper, lower)
        wsrc, tile_expert, nused, gw = _routing_tables(cnt[:, :, 0].astype(jnp.int32), nt)
        ys = _experts(wsrc, tile_expert, nused, xl, w_up[l], w_down[l], nt)
        x = _combine(gw, ys, x1, mod_l, row_of_tile, loc.T, ln2_g[l].reshape(1, D), ln2_b[l].reshape(1, D))
        new_c.append(cn_c[..., 0:DH])
        new_n.append(cn_c[..., DH])
        new_m.append(jnp.swapaxes(m_c[..., 0], 1, 2))

    return (x[0:nc].reshape(bc, tc, D), x[nc:].reshape(bl, tl, D),
            jnp.stack(new_c, axis=1), jnp.stack(new_n, axis=1), jnp.stack(new_m, axis=1))
```

```python
import functools

import numpy as np
import jax
import jax.numpy as jnp
from jax import lax
from jax.experimental import pallas as pl
from jax.experimental.pallas import tpu as pltpu

F32 = jnp.float32
BF16 = jnp.bfloat16

D = 1024
DEPTH = 4
HEADS = 4
DH = 128
A_W = HEADS * DH
B_W = 256
B_GW = 64
C_W = 256
C_GW = 64
POOL_WINDOWS = (2, 4, 8, 16)
POOL_HALO = 8
N_GROUPS = 4
EXP_PER_GROUP = 4
N_EXPERTS = 16
D_EXPERT = 512
GRID_W = 64
POS_BASE = 10000.0
LN_EPS = 1e-5
ALPHA = (2 * DEPTH) ** 0.25
K_SCALE = DH ** -0.5
N_MOD_ROWS = 8

CH = 256
TM = 256
TMM = 512
TE = 512
W_ROWS = 8
RL = 2 * TM + N_EXPERTS * W_ROWS
NSLOT = RL // W_ROWS
WPT = TE // W_ROWS
EP = 128
XW = D + 128
VMEM_LIMIT = 56 * 1024 * 1024


def _dot(a, b):
    return jnp.dot(a, b, preferred_element_type=F32)


def _dot_nt(a, b):
    return lax.dot_general(a, b, (((1,), (1,)), ((), ())), preferred_element_type=F32)


def _split2(x):
    hi = x.astype(BF16)
    lo = (x - hi.astype(F32)).astype(BF16)
    return hi, lo


def _split3(x):
    hi = x.astype(BF16)
    r = x - hi.astype(F32)
    mid = r.astype(BF16)
    lo = (r - mid.astype(F32)).astype(BF16)
    return hi, mid, lo


def _layer_norm(x, g, b):
    mu = jnp.mean(x, axis=-1, keepdims=True)
    xc = x - mu
    var = jnp.mean(xc * xc, axis=-1, keepdims=True)
    return xc * lax.rsqrt(var + LN_EPS) * g + b


def _sigmoid(x):
    return 1.0 / (1.0 + jnp.exp(-x))


def _log_sigmoid(x):
    return jnp.minimum(x, 0.0) - jnp.log1p(jnp.exp(-jnp.abs(x)))


def _params(sem):
    return pltpu.CompilerParams(dimension_semantics=sem, vmem_limit_bytes=VMEM_LIMIT)


def _ln_in_kernel(xp_ref, xs_ref, pos_ref, g_ref, b_ref, o_ref, *, n_ctx_tiles):
    i = pl.program_id(0)

    @pl.when(i < n_ctx_tiles)
    def _():
        o_ref[...] = _layer_norm(xp_ref[...], g_ref[...], b_ref[...])

    @pl.when(i >= n_ctx_tiles)
    def _():
        o_ref[...] = _layer_norm(xs_ref[...] + pos_ref[...], g_ref[...], b_ref[...])


def _ln_in(xp, xs, pos, g, b, t_lat):
    nc, ns = xp.shape[0], xs.shape[0]
    nct, nst, tps = nc // TM, ns // TM, t_lat // TM
    return pl.pallas_call(
        functools.partial(_ln_in_kernel, n_ctx_tiles=nct),
        grid=(nct + nst,),
        in_specs=[
            pl.BlockSpec((TM, D), lambda i: (jnp.minimum(i, nct - 1), 0)),
            pl.BlockSpec((TM, D), lambda i: (jnp.maximum(i - nct, 0), 0)),
            pl.BlockSpec((TM, D), lambda i: (jnp.maximum(i - nct, 0) % tps, 0)),
            pl.BlockSpec((1, D), lambda i: (0, 0)),
            pl.BlockSpec((1, D), lambda i: (0, 0)),
        ],
        out_specs=pl.BlockSpec((TM, D), lambda i: (i, 0)),
        out_shape=jax.ShapeDtypeStruct((nc + ns, D), F32),
        compiler_params=_params(("arbitrary",)),
        name="ln_in",
    )(xp, xs, pos, g, b)


def _mod_kernel(c_ref, w_ref, b_ref, o_ref):
    c = c_ref[...]
    s = c * _sigmoid(c)
    sh, sl = _split2(s)
    wh, wl = _split2(w_ref[...])
    o_ref[...] = _dot(sh, wh) + _dot(sl, wh) + _dot(sh, wl) + b_ref[...]


def _mod_all(cond, w_mod, b_mod):
    nj = 6 * D // 1024
    return pl.pallas_call(
        _mod_kernel,
        grid=(DEPTH, nj),
        in_specs=[
            pl.BlockSpec((N_MOD_ROWS, D), lambda l, j: (0, 0)),
            pl.BlockSpec((None, D, 1024), lambda l, j: (l, 0, j)),
            pl.BlockSpec((None, 1, 1024), lambda l, j: (l, 0, j)),
        ],
        out_specs=pl.BlockSpec((None, N_MOD_ROWS, 1024), lambda l, j: (l, 0, j)),
        out_shape=jax.ShapeDtypeStruct((DEPTH, N_MOD_ROWS, 6 * D), F32),
        compiler_params=_params(("arbitrary", "arbitrary")),
        name="mod",
    )(cond, w_mod, b_mod.reshape(DEPTH, 1, 6 * D))


def _inproj_kernel(x_ref, mod_ref, wqkv_ref, wkt_ref, wfp_ref, wifh_ref, wifl_ref, bif_ref,
                   qkv_ref, kt_ref, fp_ref, gi_ref, gf_ref):
    u = x_ref[...] * (1.0 + mod_ref[1:2, :]) + mod_ref[0:1, :]
    uh, ul = _split2(u)
    z = _dot(uh, wqkv_ref[...])
    qkv_ref[:, 0:A_W] = z[:, 0:A_W].astype(BF16)
    qkv_ref[:, A_W:2 * A_W] = (z[:, A_W:2 * A_W] * K_SCALE).astype(BF16)
    qkv_ref[:, 2 * A_W:3 * A_W] = z[:, 2 * A_W:3 * A_W].astype(BF16)
    kt_ref[...] = (_dot_nt(wkt_ref[...], uh) * K_SCALE).astype(BF16)
    fp_ref[...] = _dot(uh, wfp_ref[...])
    g = _dot(uh, wifh_ref[...]) + _dot(ul, wifh_ref[...]) + _dot(uh, wifl_ref[...]) + bif_ref[...]
    gi_ref[...] = g[:, 0:128]
    gf_ref[...] = g[:, 128:256]


def _inproj(x, mod_l, row_of_tile, wqkv, wkt, wfp, wifh, wifl, bif):
    n = x.shape[0]
    const = lambda i: (0, 0)
    return pl.pallas_call(
        _inproj_kernel,
        grid=(n // TMM,),
        in_specs=[
            pl.BlockSpec((TMM, D), lambda i: (i, 0)),
            pl.BlockSpec((None, 6, D), lambda i: (row_of_tile(i), 0, 0)),
            pl.BlockSpec((D, 3 * A_W), const),
            pl.BlockSpec((A_W, D), const),
            pl.BlockSpec((D, B_W + C_W), const),
            pl.BlockSpec((D, 256), const),
            pl.BlockSpec((D, 256), const),
            pl.BlockSpec((1, 256), const),
        ],
        out_specs=[
            pl.BlockSpec((TMM, 3 * A_W), lambda i: (i, 0)),
            pl.BlockSpec((A_W, TMM), lambda i: (0, i)),
            pl.BlockSpec((TMM, B_W + C_W), lambda i: (i, 0)),
            pl.BlockSpec((TMM, 128), lambda i: (i, 0)),
            pl.BlockSpec((TMM, 128), lambda i: (i, 0)),
        ],
        out_shape=[
            jax.ShapeDtypeStruct((n, 3 * A_W), BF16),
            jax.ShapeDtypeStruct((A_W, n), BF16),
            jax.ShapeDtypeStruct((n, B_W + C_W), F32),
            jax.ShapeDtypeStruct((n, 128), F32),
            jax.ShapeDtypeStruct((n, 128), F32),
        ],
        compiler_params=_params(("arbitrary",)),
        name="inproj",
    )(x, mod_l, wqkv, wkt, wfp, wifh, wifl, bif)


def _gates_kernel(gi_ref, gf_ref, git_ref, gft_ref, tri_ref, trit_ref,
                  bcol_ref, rcol_ref, rrow_ref, gsum_ref, rmx_ref):
    tri = tri_ref[...]
    trit = trit_ref[...]
    h, m, l = _split3(_log_sigmoid(gf_ref[...]))
    cum_f = _dot(tri, h) + _dot(tri, m) + _dot(tri, l)
    cum_b = _dot(trit, h) + _dot(trit, m) + _dot(trit, l)
    lane = lax.broadcasted_iota(jnp.int32, (CH, 128), 1)
    bcol = jnp.where(lane < HEADS, cum_f, cum_b)
    bcol_ref[...] = bcol
    rcol_ref[...] = gi_ref[...] - bcol

    h, m, l = _split3(_log_sigmoid(gft_ref[...]))
    row_f = _dot(h, trit) + _dot(m, trit) + _dot(l, trit)
    row_b = _dot(h, tri) + _dot(m, tri) + _dot(l, tri)
    row = lax.broadcasted_iota(jnp.int32, (8, CH), 0)
    brow = jnp.where(row < HEADS, row_f, row_b)
    rrow = git_ref[...] - brow
    rrow_ref[...] = rrow
    row1 = lax.broadcasted_iota(jnp.int32, (8, 1), 0)
    total = jnp.where(row1 < HEADS, brow[:, CH - 1:CH], brow[:, 0:1])
    gsum_ref[...] = jnp.broadcast_to(total, (8, 128))
    rmx_ref[...] = jnp.broadcast_to(jnp.max(rrow, axis=1, keepdims=True), (8, 128))


def _gates(gi, gf, git, gft, tri, trit):
    n = gi.shape[0]
    nch = n // CH
    const = lambda i: (0, 0)
    return pl.pallas_call(
        _gates_kernel,
        grid=(nch,),
        in_specs=[
            pl.BlockSpec((CH, 128), lambda i: (i, 0)),
            pl.BlockSpec((CH, 128), lambda i: (i, 0)),
            pl.BlockSpec((8, CH), lambda i: (0, i)),
            pl.BlockSpec((8, CH), lambda i: (0, i)),
            pl.BlockSpec((CH, CH), const),
            pl.BlockSpec((CH, CH), const),
        ],
        out_specs=[
            pl.BlockSpec((CH, 128), lambda i: (i, 0)),
            pl.BlockSpec((CH, 128), lambda i: (i, 0)),
            pl.BlockSpec((8, CH), lambda i: (0, i)),
            pl.BlockSpec((None, 8, 128), lambda i: (i, 0, 0)),
            pl.BlockSpec((None, 8, 128), lambda i: (i, 0, 0)),
        ],
        out_shape=[
            jax.ShapeDtypeStruct((n, 128), F32),
            jax.ShapeDtypeStruct((n, 128), F32),
            jax.ShapeDtypeStruct((8, n), F32),
            jax.ShapeDtypeStruct((nch, 8, 128), F32),
            jax.ShapeDtypeStruct((nch, 8, 128), F32),
        ],
        compiler_params=_params(("arbitrary",)),
        name="gates",
    )(gi, gf, git, gft, tri, trit)


def _mlstm_kernel(*refs, nchunk, has_state):
    if has_state:
        (q_ref, k_ref, v_ref, kt_ref, bcol_ref, rcol_ref, rrow_ref, gsum_ref, rmx_ref, lng_ref,
         cn0_ref, m0_ref, hn_ref, cn_st) = refs
    else:
        (q_ref, k_ref, v_ref, kt_ref, bcol_ref, rcol_ref, rrow_ref, gsum_ref, rmx_ref, lng_ref,
         hn_ref, cn_out_ref, m_out_ref) = refs
    head = pl.program_id(1)
    lane = lax.broadcasted_iota(jnp.int32, (CH, 128), 1)
    tpos = lax.broadcasted_iota(jnp.int32, (CH, CH), 0)
    spos = lax.broadcasted_iota(jnp.int32, (CH, CH), 1)
    masks = (spos <= tpos, spos >= tpos)

    def col(ref, j, idx):
        blk = ref[j * CH:(j + 1) * CH, :]
        return jnp.sum(jnp.where(lane == idx, blk, 0.0), axis=1, keepdims=True)

    m_in = [[None] * nchunk, [None] * nchunk]
    for d in range(2):
        idx = d * HEADS + head
        cn = cn0_ref[d] if has_state else None
        m = m0_ref[pl.ds(idx, 1), 0:1] if has_state else None
        order = range(nchunk) if d == 0 else range(nchunk - 1, -1, -1)
        for pos_in_sweep, j in enumerate(order):
            if m is not None:
                cn_st[d, j] = cn
                m_in[d][j] = m
            if has_state and pos_in_sweep == nchunk - 1:
                break
            total = gsum_ref[j, pl.ds(idx, 1), 0:1]
            rmax = rmx_ref[j, pl.ds(idx, 1), 0:1]
            mm = rmax if m is None else jnp.maximum(m, rmax)
            ws = jnp.exp(col(rcol_ref, j, idx) - mm)
            vj = v_ref[j * CH:(j + 1) * CH, :].astype(F32)
            vs = jnp.concatenate([(vj * ws).astype(BF16),
                                  jnp.where(lane == 0, ws, 0.0).astype(BF16)], axis=1)
            upd = _dot(kt_ref[:, j * CH:(j + 1) * CH], vs)
            cn = upd if m is None else jnp.exp(m - mm) * cn + upd
            m = total + mm
        if not has_state:
            cn_out_ref[d] = cn
            m_out_ref[d:d + 1, :] = jnp.broadcast_to(m, (1, 128))

    for j in range(nchunk):
        sl = slice(j * CH, (j + 1) * CH)
        qj = q_ref[sl, :]
        s_raw = _dot_nt(qj, k_ref[sl, :])
        p_sum, h_inter = None, None
        for d in range(2):
            idx = d * HEADS + head
            rr = rrow_ref[pl.ds(idx, 1), sl]
            m_row = jnp.max(jnp.where(masks[d], rr, -jnp.inf), axis=1, keepdims=True)
            m_prev = m_in[d][j]
            if m_prev is not None:
                m_row = jnp.maximum(m_row, m_prev)
            p = s_raw * jnp.where(masks[d], jnp.exp(rr - m_row), 0.0)
            den = jnp.sum(p, axis=1, keepdims=True)
            if m_prev is not None:
                inter = jnp.exp(m_prev - m_row)
                qc = _dot(qj, cn_st[d, j].astype(BF16))
                den = den + inter * qc[:, DH:DH + 1]
            floor = jnp.exp(-(col(bcol_ref, j, idx) + m_row))
            rinv = 1.0 / jnp.maximum(jnp.abs(den), floor)
            p_sum = p * rinv if p_sum is None else p_sum + p * rinv
            if m_prev is not None:
                hi = (inter * rinv) * qc[:, 0:DH]
                h_inter = hi if h_inter is None else h_inter + hi
        hj = _dot(p_sum.astype(BF16), v_ref[sl, :])
        if h_inter is not None:
            hj = hj + h_inter
        mu = jnp.mean(hj, axis=-1, keepdims=True)
        hc = hj - mu
        var = jnp.mean(hc * hc, axis=-1, keepdims=True)
        hn_ref[sl, :] = hc * lax.rsqrt(var + LN_EPS) * lng_ref[...]


def _mlstm(qkv, kt, bcol, rcol, rrow, gsum, rmx, lng, tok0, nseq, t, cn0=None, m0=None):
    nchunk = t // CH
    sb = tok0 // t
    cb = tok0 // CH
    has_state = cn0 is not None
    in_specs = [
        pl.BlockSpec((t, DH), lambda s, h: (sb + s, h)),
        pl.BlockSpec((t, DH), lambda s, h: (sb + s, HEADS + h)),
        pl.BlockSpec((t, DH), lambda s, h: (sb + s, 2 * HEADS + h)),
        pl.BlockSpec((DH, t), lambda s, h: (h, sb + s)),
        pl.BlockSpec((t, 128), lambda s, h: (sb + s, 0)),
        pl.BlockSpec((t, 128), lambda s, h: (sb + s, 0)),
        pl.BlockSpec((8, t), lambda s, h: (0, sb + s)),
        pl.BlockSpec((nchunk, 8, 128), lambda s, h: (cb // nchunk + s, 0, 0)),
        pl.BlockSpec((nchunk, 8, 128), lambda s, h: (cb // nchunk + s, 0, 0)),
        pl.BlockSpec((1, DH), lambda s, h: (0, h)),
    ]
    args = [qkv, qkv, qkv, kt, bcol, rcol, rrow, gsum, rmx, lng]
    hn_spec = pl.BlockSpec((t, DH), lambda s, h: (s, h))
    hn_shape = jax.ShapeDtypeStruct((nseq * t, A_W), F32)
    if has_state:
        in_specs += [
            pl.BlockSpec((None, 2, None, DH, 2 * DH), lambda s, h: (s, 0, h, 0, 0)),
            pl.BlockSpec((None, 8, 128), lambda s, h: (s, 0, 0)),
        ]
        args += [cn0, m0]
        out_specs, out_shape = hn_spec, hn_shape
        scratch = [pltpu.VMEM((2, nchunk, DH, 2 * DH), F32)]
    else:
        out_specs = [
            hn_spec,
            pl.BlockSpec((None, 2, None, DH, 2 * DH), lambda s, h: (s, 0, h, 0, 0)),
            pl.BlockSpec((None, None, 2, 128), lambda s, h: (s, h, 0, 0)),
        ]
        out_shape = [
            hn_shape,
            jax.ShapeDtypeStruct((nseq, 2, HEADS, DH, 2 * DH), F32),
            jax.ShapeDtypeStruct((nseq, HEADS, 2, 128), F32),
        ]
        scratch = []
    return pl.pallas_call(
        functools.partial(_mlstm_kernel, nchunk=nchunk, has_state=has_state),
        grid=(nseq, HEADS),
        in_specs=in_specs,
        out_specs=out_specs,
        out_shape=out_shape,
        scratch_shapes=scratch,
        compiler_params=_params(("arbitrary", "arbitrary")),
        name="mlstm_lat" if has_state else "mlstm_ctx",
    )(*args)


def _fftpool_kernel(fp_ref, cth_ref, ctl_ref, sth_ref, stl_ref, bcsh_ref, bcsl_ref, icnt_ref,
                    wpool_ref, pscale_ref, yf_ref, pp_ref, pad_ref, *, t):
    fh, fl = _split2(fp_ref[:, 0:B_W])
    xcs = _dot(fh, bcsh_ref[...]) + _dot(fl, bcsh_ref[...]) + _dot(fh, bcsl_ref[...])
    xch, xcl = _split2(xcs[:, 0:B_W])
    xsh, xsl = _split2(xcs[:, B_W:2 * B_W])
    y = (_dot(cth_ref[...], xch) + _dot(ctl_ref[...], xch) + _dot(cth_ref[...], xcl)
         - _dot(sth_ref[...], xsh) - _dot(stl_ref[...], xsh) - _dot(sth_ref[...], xsl))
    yf_ref[...] = (y * ((t * B_GW) ** -0.5)).astype(BF16)

    p = fp_ref[:, B_W:B_W + C_W]
    zero = jnp.zeros((POOL_HALO, C_W), F32)
    pad_ref[0:POOL_HALO, :] = zero
    pad_ref[POOL_HALO + t:2 * POOL_HALO + t, :] = zero
    pad_ref[POOL_HALO:POOL_HALO + t, :] = p

    def sh(dlt):
        return pad_ref[POOL_HALO + dlt:POOL_HALO + dlt + t, :]

    w2 = sh(-1) + p
    w4 = w2 + sh(-2) + sh(1)
    w8 = w4 + sh(-4) + sh(-3) + sh(2) + sh(3)
    w16 = w8 + sh(-8) + sh(-7) + sh(-6) + sh(-5) + sh(4) + sh(5) + sh(6) + sh(7)
    grp = lax.broadcasted_iota(jnp.int32, (t, C_W), 1) // C_GW
    wsum = jnp.where(grp == 0, w2, jnp.where(grp == 1, w4, jnp.where(grp == 2, w8, w16)))
    pooled = wsum * icnt_ref[...] - p
    pp_ref[...] = (_dot(pooled.astype(BF16), wpool_ref[...]) * pscale_ref[...]).astype(BF16)


def _fftpool(fp, consts, wpool_bd, pscale, tok0, nseq, t):
    cth, ctl, sth, stl, bcsh, bcsl, icnt = consts
    sb = tok0 // t
    const = lambda s: (0, 0)
    return pl.pallas_call(
        functools.partial(_fftpool_kernel, t=t),
        grid=(nseq,),
        in_specs=[
            pl.BlockSpec((t, B_W + C_W), lambda s: (sb + s, 0)),
            pl.BlockSpec((t, t), const), pl.BlockSpec((t, t), const),
            pl.BlockSpec((t, t), const), pl.BlockSpec((t, t), const),
            pl.BlockSpec((B_W, 2 * B_W), const), pl.BlockSpec((B_W, 2 * B_W), const),
            pl.BlockSpec((t, C_W), const),
            pl.BlockSpec((C_W, C_W), const),
            pl.BlockSpec((1, C_W), const),
        ],
        out_specs=[pl.BlockSpec((t, B_W), lambda s: (s, 0)), pl.BlockSpec((t, C_W), lambda s: (s, 0))],
        out_shape=[jax.ShapeDtypeStruct((nseq * t, B_W), BF16), jax.ShapeDtypeStruct((nseq * t, C_W), BF16)],
        scratch_shapes=[pltpu.VMEM((t + 2 * POOL_HALO, C_W), F32)],
        compiler_params=_params(("arbitrary",)),
        name=f"fftpool_{t}",
    )(fp, cth, ctl, sth, stl, bcsh, bcsl, icnt, wpool_bd, pscale)


def _merge_kernel(x_ref, mod_ref, hn_ref, yf_ref, pp_ref, wo_ref, wg_ref, wpa_ref, wpb_ref, wpc_ref,
                  wout_ref, g1_ref, b1_ref, wrh_ref, wrl_ref, br_ref, x1_ref, u2_ref, route_ref):
    x = x_ref[...]
    ub = (x * (1.0 + mod_ref[1:2, :]) + mod_ref[0:1, :]).astype(BF16)
    a = (hn_ref[...] * _sigmoid(_dot(ub, wo_ref[...]))).astype(BF16)
    acc = _sigmoid(_dot(ub, wg_ref[:, 0:D])) * _dot(a, wpa_ref[...])
    acc = acc + _sigmoid(_dot(ub, wg_ref[:, D:2 * D])) * _dot(yf_ref[...], wpb_ref[...])
    acc = acc + _sigmoid(_dot(ub, wg_ref[:, 2 * D:3 * D])) * _dot(pp_ref[...], wpc_ref[...])
    out = _dot(acc.astype(BF16), wout_ref[...])
    x1 = _layer_norm(ALPHA * x + mod_ref[2:3, :] * out, g1_ref[...], b1_ref[...])
    x1_ref[...] = x1
    u2 = x1 * (1.0 + mod_ref[4:5, :]) + mod_ref[3:4, :]
    u2_ref[...] = u2.astype(BF16)

    uh, ul = _split2(u2)
    lt = _dot_nt(wrh_ref[...], uh) + _dot_nt(wrh_ref[...], ul) + _dot_nt(wrl_ref[...], uh) + br_ref[:, 0:1]
    gl = [lt[g:g + 1, :] for g in range(N_GROUPS)]
    gmax = jnp.maximum(jnp.maximum(gl[0], gl[1]), jnp.maximum(gl[2], gl[3]))
    gidx = jnp.where(gl[0] >= gmax, 0, jnp.where(gl[1] >= gmax, 1, jnp.where(gl[2] >= gmax, 2, 3)))
    gval = 1.0 / (jnp.exp(gl[0] - gmax) + jnp.exp(gl[1] - gmax) + jnp.exp(gl[2] - gmax) + jnp.exp(gl[3] - gmax))
    fe = []
    for e in range(EXP_PER_GROUP):
        r = [lt[N_GROUPS + g * EXP_PER_GROUP + e:N_GROUPS + g * EXP_PER_GROUP + e + 1, :] for g in range(N_GROUPS)]
        fe.append(jnp.where(gidx == 0, r[0], jnp.where(gidx == 1, r[1], jnp.where(gidx == 2, r[2], r[3]))))

    def top1(v):
        vmax = jnp.maximum(jnp.maximum(v[0], v[1]), jnp.maximum(v[2], v[3]))
        imax = jnp.where(v[0] >= vmax, 0, jnp.where(v[1] >= vmax, 1, jnp.where(v[2] >= vmax, 2, 3)))
        return vmax, imax

    v1, i1 = top1(fe)
    v2, i2 = top1([jnp.where(i1 == e, -jnp.inf, fe[e]) for e in range(EXP_PER_GROUP)])
    tt = jnp.exp(v2 - v1)
    w1 = gval / (1.0 + tt)
    w2 = gval * tt / (1.0 + tt)
    e1 = (gidx * EXP_PER_GROUP + i1).astype(F32)
    e2 = (gidx * EXP_PER_GROUP + i2).astype(F32)
    route_ref[...] = jnp.zeros(route_ref.shape, F32)
    route_ref[0:1, :] = e1
    route_ref[1:2, :] = e2
    route_ref[2:3, :] = w1
    route_ref[3:4, :] = w2


def _merge(x, mod_l, row_of_tile, hn, yf, pp, wo, wg, wpa, wpb, wpc, wout, g1, b1, wrh, wrl, br):
    n = x.shape[0]
    const = lambda i: (0, 0)
    tok = lambda i: (i, 0)
    return pl.pallas_call(
        _merge_kernel,
        grid=(n // TMM,),
        in_specs=[
            pl.BlockSpec((TMM, D), tok),
            pl.BlockSpec((None, 6, D), lambda i: (row_of_tile(i), 0, 0)),
            pl.BlockSpec((TMM, A_W), tok),
            pl.BlockSpec((TMM, B_W), tok),
            pl.BlockSpec((TMM, C_W), tok),
            pl.BlockSpec((D, A_W), const),
            pl.BlockSpec((D, 3 * D), const),
            pl.BlockSpec((A_W, D), const),
            pl.BlockSpec((B_W, D), const),
            pl.BlockSpec((C_W, D), const),
            pl.BlockSpec((D, D), const),
            pl.BlockSpec((1, D), const),
            pl.BlockSpec((1, D), const),
            pl.BlockSpec((32, D), const),
            pl.BlockSpec((32, D), const),
            pl.BlockSpec((32, 128), const),
        ],
        out_specs=[pl.BlockSpec((TMM, D), tok), pl.BlockSpec((TMM, D), tok), pl.BlockSpec((8, TMM), lambda i: (0, i))],
        out_shape=[jax.ShapeDtypeStruct((n, D), F32), jax.ShapeDtypeStruct((n, D), BF16),
                   jax.ShapeDtypeStruct((8, n), F32)],
        compiler_params=_params(("arbitrary",)),
        name="merge",
    )(x, mod_l, hn, yf, pp, wo, wg, wpa, wpb, wpc, wout, g1, b1, wrh, wrl, br)


def _lsort_kernel(route_ref, u2_ref, upper_ref, lower_ref, xl_ref, cnt_ref, loc_ref):
    e0 = route_ref[0:1, :].astype(jnp.int32)
    e1 = route_ref[1:2, :].astype(jnp.int32)
    eid = lax.broadcasted_iota(jnp.int32, (EP, TM), 0)
    m0 = eid == e0
    m1 = eid == e1
    member = jnp.where(m0 | m1, 1.0, 0.0)
    rank = _dot(member.astype(BF16), upper_ref[...])
    cnt = jnp.sum(member, axis=1, keepdims=True)
    nwin = jnp.floor((cnt + (W_ROWS - 1)) * (1.0 / W_ROWS))
    first = _dot(lower_ref[...], jnp.broadcast_to(nwin, (EP, 128)).astype(BF16))[:, 0:1] * W_ROWS
    r0 = jnp.sum(jnp.where(m0, first + rank, 0.0), axis=0, keepdims=True).astype(jnp.int32)
    r1 = jnp.sum(jnp.where(m1, first + rank, 0.0), axis=0, keepdims=True).astype(jnp.int32)
    rid = lax.broadcasted_iota(jnp.int32, (RL, TM), 0)
    h0 = rid == r0
    h1 = rid == r1
    xl_ref[:, 0:D] = _dot(jnp.where(h0 | h1, 1.0, 0.0).astype(BF16), u2_ref[...])
    wsel = jnp.where(h0, route_ref[2:3, :], 0.0) + jnp.where(h1, route_ref[3:4, :], 0.0)
    xl_ref[:, D:XW] = jnp.broadcast_to(jnp.sum(wsel, axis=1, keepdims=True), (RL, XW - D))
    cnt_ref[...] = jnp.broadcast_to(cnt[0:N_EXPERTS, :], (N_EXPERTS, 128))
    loc_ref[...] = jnp.zeros(loc_ref.shape, F32)
    loc_ref[0:1, :] = r0.astype(F32)
    loc_ref[1:2, :] = r1.astype(F32)


def _lsort(route, u2, upper, lower):
    n = u2.shape[0]
    nb = n // TM
    const = lambda i: (0, 0)
    return pl.pallas_call(
        _lsort_kernel,
        grid=(nb,),
        in_specs=[
            pl.BlockSpec((8, TM), lambda i: (0, i)),
            pl.BlockSpec((TM, D), lambda i: (i, 0)),
            pl.BlockSpec((TM, TM), const),
            pl.BlockSpec((EP, EP), const),
        ],
        out_specs=[
            pl.BlockSpec((RL, XW), lambda i: (i, 0)),
            pl.BlockSpec((None, N_EXPERTS, 128), lambda i: (i, 0, 0)),
            pl.BlockSpec((8, TM), lambda i: (0, i)),
        ],
        out_shape=[
            jax.ShapeDtypeStruct((nb * RL, XW), F32),
            jax.ShapeDtypeStruct((nb, N_EXPERTS, 128), F32),
            jax.ShapeDtypeStruct((8, n), F32),
        ],
        compiler_params=_params(("arbitrary",)),
        name="lsort",
    )(route, u2, upper, lower)


def _fetch_windows(src_hbm, rows_ref, first, n_windows, dst, sem):
    def body(w, c):
        src = pl.ds(pl.multiple_of(rows_ref[first + w], W_ROWS), W_ROWS)
        pltpu.make_async_copy(src_hbm.at[src], dst.at[pl.ds(pl.multiple_of(w * W_ROWS, W_ROWS), W_ROWS)], sem).start()
        return c

    lax.fori_loop(0, n_windows, body, 0, unroll=8)


def _wait_windows(src_hbm, dst, sem):
    pltpu.make_async_copy(src_hbm.at[pl.ds(0, dst.shape[0])], dst, sem).wait()


def _experts_kernel(wsrc_ref, te_ref, nused_ref, xl_hbm, wup_ref, wdn_ref, ys_ref, xbuf, sems):
    j = pl.program_id(0)
    nused = nused_ref[0]
    slot = j % 2

    @pl.when(j == 0)
    def _():
        _fetch_windows(xl_hbm, wsrc_ref, 0, WPT, xbuf.at[0], sems.at[0])

    @pl.when(j + 1 < nused)
    def _():
        _fetch_windows(xl_hbm, wsrc_ref, (j + 1) * WPT, WPT, xbuf.at[1 - slot], sems.at[1 - slot])

    @pl.when(j < nused)
    def _():
        _wait_windows(xl_hbm, xbuf.at[slot], sems.at[slot])
        xw = xbuf[slot]
        hid = _dot(xw[:, 0:D].astype(BF16), wup_ref[...].astype(BF16))
        a = hid[:, 0:D_EXPERT]
        act = a * _sigmoid(a) * hid[:, D_EXPERT:2 * D_EXPERT] * xw[:, D:D + 1]
        ys_ref[...] = _dot(act.astype(BF16), wdn_ref[...].astype(BF16))

    @pl.when(j >= nused)
    def _():
        ys_ref[...] = jnp.zeros(ys_ref.shape, F32)


def _experts(wsrc, tile_expert, nused, xl, w_up_l, w_down_l, nt):
    return pl.pallas_call(
        _experts_kernel,
        grid_spec=pltpu.PrefetchScalarGridSpec(
            num_scalar_prefetch=3,
            grid=(nt,),
            in_specs=[
                pl.BlockSpec(memory_space=pl.ANY),
                pl.BlockSpec((None, D, 2 * D_EXPERT), lambda i, ws, te, nu: (te[i], 0, 0)),
                pl.BlockSpec((None, D_EXPERT, D), lambda i, ws, te, nu: (te[i], 0, 0)),
            ],
            out_specs=pl.BlockSpec((TE, D), lambda i, ws, te, nu: (i, 0)),
            scratch_shapes=[pltpu.VMEM((2, TE, XW), F32), pltpu.SemaphoreType.DMA((2,))],
        ),
        out_shape=jax.ShapeDtypeStruct((nt * TE, D), F32),
        compiler_params=_params(("arbitrary",)),
        name="experts",
    )(wsrc, tile_expert, nused, xl, w_up_l, w_down_l)


def _combine_kernel(gw_ref, ys_hbm, x1_ref, mod_ref, loc_ref, g2_ref, b2_ref, o_ref, ybuf, sems):
    i = pl.program_id(0)
    slot = i % 2

    @pl.when(i == 0)
    def _():
        _fetch_windows(ys_hbm, gw_ref, 0, NSLOT, ybuf.at[0], sems.at[0])

    @pl.when(i + 1 < pl.num_programs(0))
    def _():
        _fetch_windows(ys_hbm, gw_ref, (i + 1) * NSLOT, NSLOT, ybuf.at[1 - slot], sems.at[1 - slot])

    _wait_windows(ys_hbm, ybuf.at[slot], sems.at[slot])
    r0 = loc_ref[:, 0:1].astype(jnp.int32)
    r1 = loc_ref[:, 1:2].astype(jnp.int32)
    lid = lax.broadcasted_iota(jnp.int32, (TM, RL), 1)
    pt = jnp.where((lid == r0) | (lid == r1), 1.0, 0.0).astype(BF16)
    yh, yl = _split2(ybuf[slot])
    y = _dot(pt, yh) + _dot(pt, yl)
    o_ref[...] = _layer_norm(ALPHA * x1_ref[...] + mod_ref[5:6, :] * y, g2_ref[...], b2_ref[...])


def _combine(gw, ys, x1, mod_l, row_of_tile, loc_t, g2, b2):
    n = x1.shape[0]
    return pl.pallas_call(
        _combine_kernel,
        grid_spec=pltpu.PrefetchScalarGridSpec(
            num_scalar_prefetch=1,
            grid=(n // TM,),
            in_specs=[
                pl.BlockSpec(memory_space=pl.ANY),
                pl.BlockSpec((TM, D), lambda i, gw: (i, 0)),
                pl.BlockSpec((None, 6, D), lambda i, gw: (row_of_tile(i), 0, 0)),
                pl.BlockSpec((TM, 8), lambda i, gw: (i, 0)),
                pl.BlockSpec((1, D), lambda i, gw: (0, 0)),
                pl.BlockSpec((1, D), lambda i, gw: (0, 0)),
            ],
            out_specs=pl.BlockSpec((TM, D), lambda i, gw: (i, 0)),
            scratch_shapes=[pltpu.VMEM((2, RL, D), F32), pltpu.SemaphoreType.DMA((2,))],
        ),
        out_shape=jax.ShapeDtypeStruct((n, D), F32),
        compiler_params=_params(("arbitrary",)),
        name="combine",
    )(gw, ys, x1, mod_l, loc_t, g2, b2)


def _routing_tables(cnt, nt):
    nb = cnt.shape[0]
    nwin = (cnt + W_ROWS - 1) // W_ROWS
    lo_inc = jnp.cumsum(nwin, axis=1)
    lo = lo_inc - nwin
    bo_inc = jnp.cumsum(nwin, axis=0)
    bo = bo_inc - nwin
    we = bo_inc[-1]
    wpad = ((we + WPT - 1) // WPT) * WPT
    eend = jnp.cumsum(wpad)
    eoff = eend - wpad
    nused = (eend[-1] // WPT).reshape(1)
    tile_expert = jnp.minimum(
        jnp.sum((jnp.arange(nt, dtype=jnp.int32)[:, None] * WPT >= eend[None, :]).astype(jnp.int32), axis=1),
        N_EXPERTS - 1)
    experts = jnp.arange(N_EXPERTS, dtype=jnp.int32)
    pick = lambda onehot, table: jnp.sum(jnp.where(onehot, table, 0), axis=-1)
    g = jnp.arange(nt * WPT, dtype=jnp.int32)
    oh_e = jnp.repeat(tile_expert[:, None] == experts[None, :], WPT, axis=0)
    ig = g - pick(oh_e, eoff[None, :])
    binc_g = pick(oh_e[:, None, :], bo_inc[None, :, :])
    bg = jnp.minimum(jnp.sum((binc_g <= ig[:, None]).astype(jnp.int32), axis=1), nb - 1)
    oh_b = bg[:, None] == jnp.arange(nb, dtype=jnp.int32)[None, :]
    shift_g = pick(oh_b, pick(oh_e[:, None, :], (lo - bo)[None, :, :]))
    empty_window_row = RL - W_ROWS
    wsrc = jnp.where(ig < pick(oh_e, we[None, :]), bg * RL + (shift_g + ig) * W_ROWS, empty_window_row)
    s = jnp.arange(NSLOT, dtype=jnp.int32)
    es = jnp.minimum(jnp.sum((lo_inc[:, None, :] <= s[None, :, None]).astype(jnp.int32), axis=2), N_EXPERTS - 1)
    oh_s = es[:, :, None] == experts[None, None, :]
    gwin = pick(oh_s, (eoff[None, :] + bo - lo)[:, None, :]) + s[None, :]
    gw = jnp.where(s[None, :] < lo_inc[:, -1:], gwin, 0) * W_ROWS
    i32 = lambda a: a.astype(jnp.int32)
    return i32(wsrc), i32(tile_expert), i32(nused), i32(gw).reshape(-1)


def _grid_pos_embed(n_tok):
    rows = n_tok // GRID_W
    r, col = np.meshgrid(np.arange(rows, dtype=np.float32), np.arange(GRID_W, dtype=np.float32), indexing="ij")
    r, col = r.reshape(-1), col.reshape(-1)
    quarter = D // 4
    freq = (1.0 / (np.float32(POS_BASE) ** (np.arange(quarter, dtype=np.float32) / np.float32(quarter)))).astype(np.float32)
    ang_r = r[:, None] * freq[None, :]
    ang_c = col[:, None] * freq[None, :]
    return jnp.asarray(np.concatenate([np.sin(ang_r), np.cos(ang_r), np.sin(ang_c), np.cos(ang_c)], axis=-1),
                       dtype=F32)


def _dft_consts(t):
    def hl(a):
        hi = a.astype(BF16)
        lo = (a - hi.astype(np.float64)).astype(BF16)
        return jnp.asarray(hi), jnp.asarray(lo)

    tt = np.arange(t)
    ang = 2.0 * np.pi * ((tt[:, None] * tt[None, :]) % t) / t
    cth, ctl = hl(np.cos(ang))
    sth, stl = hl(np.sin(ang))
    cc = np.arange(B_GW)
    angc = 2.0 * np.pi * ((cc[:, None] * cc[None, :]) % B_GW) / B_GW
    eye = np.eye(B_W // B_GW)
    bcs = np.concatenate([np.kron(eye, np.cos(angc)), np.kron(eye, np.sin(angc))], axis=1)
    bcsh, bcsl = hl(bcs)
    pos = np.arange(t)
    icnt = np.concatenate(
        [np.repeat((1.0 / (np.minimum(pos + w // 2, t) - np.maximum(pos - w // 2, 0)))[:, None], C_GW, axis=1)
         for w in POOL_WINDOWS], axis=1)
    return cth, ctl, sth, stl, bcsh, bcsl, jnp.asarray(icnt, F32)


def kernel(x_prompt, x_sample, c, state_C, state_n, state_m, c_ctx, ln_in_g, ln_in_b, w_mod, b_mod, w_in, b_if, ln_a_g, w_pa, w_pb, w_pc, w_pool, pool_scale, w_out, ln1_g, ln1_b, w_r1, b_r1, w_r2, b_r2, w_up, w_down, ln2_g, ln2_b):
    bc, tc, _ = x_prompt.shape
    bl, tl, _ = x_sample.shape
    nc, nl = bc * tc, bl * tl
    n = nc + nl
    assert tc == CH and tl % CH == 0 and nc % tl == 0 and bl + 1 <= N_MOD_ROWS
    nt = -(-(2 * n // W_ROWS + (n // TM) * N_EXPERTS + N_EXPERTS * (WPT - 1)) // WPT)
    upper = jnp.asarray(np.triu(np.ones((TM, TM), np.float32), 1), BF16)
    lower = jnp.asarray(np.tril(np.ones((EP, EP), np.float32), -1), BF16)

    def row_fn(tile):
        return lambda i: jnp.where(i < nc // tile, 0, 1 + jnp.maximum(i - nc // tile, 0) // (tl // tile))

    row_of_tile, row_of_proj_tile = row_fn(TM), row_fn(TMM)

    wqkv = w_in[:, :, 0:3 * A_W].astype(BF16)
    wkt = jnp.swapaxes(w_in[:, :, A_W:2 * A_W], 1, 2).astype(BF16)
    wo = w_in[:, :, 3 * A_W:4 * A_W].astype(BF16)
    c0 = 4 * A_W
    wif = w_in[:, :, c0:c0 + 16].reshape(DEPTH, D, 2, 2, HEADS)
    zpad = jnp.zeros((DEPTH, D, 128 - 2 * HEADS), F32)
    wif = jnp.concatenate([wif[:, :, :, 0, :].reshape(DEPTH, D, 2 * HEADS), zpad,
                           wif[:, :, :, 1, :].reshape(DEPTH, D, 2 * HEADS), zpad], axis=2)
    wifh, wifl = _split2(wif)
    bpad = jnp.zeros((DEPTH, 128 - 2 * HEADS), F32)
    bif = jnp.concatenate([b_if[:, :, 0, :].reshape(DEPTH, 2 * HEADS), bpad,
                           b_if[:, :, 1, :].reshape(DEPTH, 2 * HEADS), bpad], axis=1).reshape(DEPTH, 1, 256)
    wfp = w_in[:, :, c0 + 16:c0 + 16 + B_W + C_W].astype(BF16)
    wg = w_in[:, :, c0 + 16 + B_W + C_W:].astype(BF16)
    wpa, wpb, wpc, wout = (w.astype(BF16) for w in (w_pa, w_pb, w_pc, w_out))
    eye = jnp.eye(C_W // C_GW, dtype=F32)
    wpool_bd = jnp.einsum("gh,lgce->lgche", eye, w_pool).reshape(DEPTH, C_W, C_W).astype(BF16)
    wr = jnp.concatenate([w_r1, w_r2.reshape(DEPTH, D, N_EXPERTS),
                          jnp.zeros((DEPTH, D, 32 - N_GROUPS - N_EXPERTS), F32)], axis=2)
    wrh, wrl = _split2(jnp.swapaxes(wr, 1, 2))
    br = jnp.concatenate([b_r1, b_r2.reshape(DEPTH, N_EXPERTS),
                          jnp.zeros((DEPTH, 32 - N_GROUPS - N_EXPERTS), F32)], axis=1)
    br = jnp.broadcast_to(br[:, :, None], (DEPTH, 32, 128))

    tri_np = np.tril(np.ones((CH, CH), np.float32))
    tri, trit = jnp.asarray(tri_np, BF16), jnp.asarray(tri_np.T, BF16)
    consts_c, consts_l = _dft_consts(tc), _dft_consts(tl)

    cond = jnp.concatenate([c_ctx[None, :], c, jnp.zeros((N_MOD_ROWS - 1 - bl, D), F32)], axis=0)
    mod = _mod_all(cond, w_mod, b_mod).reshape(DEPTH, N_MOD_ROWS, 6, D)

    x = _ln_in(x_prompt.reshape(nc, D), x_sample.reshape(nl, D), _grid_pos_embed(tl),
               ln_in_g.reshape(1, D), ln_in_b.reshape(1, D), tl)

    cn0 = jnp.concatenate([state_C, state_n[..., None], jnp.zeros(state_C.shape[:-1] + (DH - 1,), F32)], axis=-1)
    m0 = jnp.broadcast_to(state_m.reshape(bl, DEPTH, 2 * HEADS, 1), (bl, DEPTH, 2 * HEADS, 128))

    new_c, new_n, new_m = [], [], []
    for l in range(DEPTH):
        mod_l = mod[l]
        qkv, kt, fp, gi, gf = _inproj(x, mod_l, row_of_proj_tile, wqkv[l], wkt[l], wfp[l], wifh[l], wifl[l], bif[l])
        bcol, rcol, rrow, gsum, rmx = _gates(gi, gf, gi[:, 0:8].T, gf[:, 0:8].T, tri, trit)
        lng = ln_a_g[l].reshape(1, A_W)
        hn_c, cn_c, m_c = _mlstm(qkv, kt, bcol, rcol, rrow, gsum, rmx, lng, 0, bc, tc)
        hn_l = _mlstm(qkv, kt, bcol, rcol, rrow, gsum, rmx, lng, nc, bl, tl, cn0[:, l], m0[:, l])
        pscale = pool_scale[l].reshape(1, C_W)
        yf_c, pp_c = _fftpool(fp, consts_c, wpool_bd[l], pscale, 0, bc, tc)
        yf_l, pp_l = _fftpool(fp, consts_l, wpool_bd[l], pscale, nc, bl, tl)
        hn = jnp.concatenate([hn_c, hn_l], axis=0)
        yf = jnp.concatenate([yf_c, yf_l], axis=0)
        pp = jnp.concatenate([pp_c, pp_l], axis=0)
        x1, u2, route = _merge(x, mod_l, row_of_proj_tile, hn, yf, pp, wo[l], wg[l], wpa[l], wpb[l], wpc[l], wout[l],
                               ln1_g[l].reshape(1, D), ln1_b[l].reshape(1, D), wrh[l], wrl[l], br[l])
        xl, cnt, loc = _lsort(route, u2, upper, lower)
        wsrc, tile_expert, nused, gw = _routing_tables(cnt[:, :, 0].astype(jnp.int32), nt)
        ys = _experts(wsrc, tile_expert, nused, xl, w_up[l], w_down[l], nt)
        x = _combine(gw, ys, x1, mod_l, row_of_tile, loc.T, ln2_g[l].reshape(1, D), ln2_b[l].reshape(1, D))
        new_c.append(cn_c[..., 0:DH])
        new_n.append(cn_c[..., DH])
        new_m.append(jnp.swapaxes(m_c[..., 0], 1, 2))

    return (x[0:nc].reshape(bc, tc, D), x[nc:].reshape(bl, tl, D),
            jnp.stack(new_c, axis=1), jnp.stack(new_n, axis=1), jnp.stack(new_m, axis=1))
```

```python
import functools

import numpy as np
import jax
import jax.numpy as jnp
from jax import lax
from jax.experimental import pallas as pl
from jax.experimental.pallas import tpu as pltpu

F32 = jnp.float32
BF16 = jnp.bfloat16

D = 1024
DEPTH = 4
HEADS = 4
DH = 128
A_W = HEADS * DH
B_W = 256
B_GW = 64
C_W = 256
C_GW = 64
POOL_WINDOWS = (2, 4, 8, 16)
POOL_HALO = 8
N_GROUPS = 4
EXP_PER_GROUP = 4
N_EXPERTS = 16
D_EXPERT = 512
GRID_W = 64
POS_BASE = 10000.0
LN_EPS = 1e-5
ALPHA = (2 * DEPTH) ** 0.25
K_SCALE = DH ** -0.5
N_MOD_ROWS = 8

CH = 256
TM = 256
TMM = 512
TE = 512
W_ROWS = 8
RL = 2 * TM + N_EXPERTS * W_ROWS
NSLOT = RL // W_ROWS
WPT = TE // W_ROWS
EP = 128
XW = D + 128
VMEM_LIMIT = 56 * 1024 * 1024


def _dot(a, b):
    return jnp.dot(a, b, preferred_element_type=F32)


def _dot_nt(a, b):
    return lax.dot_general(a, b, (((1,), (1,)), ((), ())), preferred_element_type=F32)


def _split2(x):
    hi = x.astype(BF16)
    lo = (x - hi.astype(F32)).astype(BF16)
    return hi, lo


def _split3(x):
    hi = x.astype(BF16)
    r = x - hi.astype(F32)
    mid = r.astype(BF16)
    lo = (r - mid.astype(F32)).astype(BF16)
    return hi, mid, lo


def _layer_norm(x, g, b):
    mu = jnp.mean(x, axis=-1, keepdims=True)
    xc = x - mu
    var = jnp.mean(xc * xc, axis=-1, keepdims=True)
    return xc * lax.rsqrt(var + LN_EPS) * g + b


def _sigmoid(x):
    return 1.0 / (1.0 + jnp.exp(-x))


def _log_sigmoid(x):
    return jnp.minimum(x, 0.0) - jnp.log1p(jnp.exp(-jnp.abs(x)))


def _params(sem):
    return pltpu.CompilerParams(dimension_semantics=sem, vmem_limit_bytes=VMEM_LIMIT)


def _ln_in_kernel(xp_ref, xs_ref, pos_ref, g_ref, b_ref, o_ref, *, n_ctx_tiles):
    i = pl.program_id(0)

    @pl.when(i < n_ctx_tiles)
    def _():
        o_ref[...] = _layer_norm(xp_ref[...], g_ref[...], b_ref[...])

    @pl.when(i >= n_ctx_tiles)
    def _():
        o_ref[...] = _layer_norm(xs_ref[...] + pos_ref[...], g_ref[...], b_ref[...])


def _ln_in(xp, xs, pos, g, b, t_lat):
    nc, ns = xp.shape[0], xs.shape[0]
    nct, nst, tps = nc // TM, ns // TM, t_lat // TM
    return pl.pallas_call(
        functools.partial(_ln_in_kernel, n_ctx_tiles=nct),
        grid=(nct + nst,),
        in_specs=[
            pl.BlockSpec((TM, D), lambda i: (jnp.minimum(i, nct - 1), 0)),
            pl.BlockSpec((TM, D), lambda i: (jnp.maximum(i - nct, 0), 0)),
            pl.BlockSpec((TM, D), lambda i: (jnp.maximum(i - nct, 0) % tps, 0)),
            pl.BlockSpec((1, D), lambda i: (0, 0)),
            pl.BlockSpec((1, D), lambda i: (0, 0)),
        ],
        out_specs=pl.BlockSpec((TM, D), lambda i: (i, 0)),
        out_shape=jax.ShapeDtypeStruct((nc + ns, D), F32),
        compiler_params=_params(("arbitrary",)),
        name="ln_in",
    )(xp, xs, pos, g, b)


def _mod_kernel(c_ref, w_ref, b_ref, o_ref):
    c = c_ref[...]
    s = c * _sigmoid(c)
    sh, sl = _split2(s)
    wh, wl = _split2(w_ref[...])
    o_ref[...] = _dot(sh, wh) + _dot(sl, wh) + _dot(sh, wl) + b_ref[...]


def _mod_all(cond, w_mod, b_mod):
    nj = 6 * D // 1024
    return pl.pallas_call(
        _mod_kernel,
        grid=(DEPTH, nj),
        in_specs=[
            pl.BlockSpec((N_MOD_ROWS, D), lambda l, j: (0, 0)),
            pl.BlockSpec((None, D, 1024), lambda l, j: (l, 0, j)),
            pl.BlockSpec((None, 1, 1024), lambda l, j: (l, 0, j)),
        ],
        out_specs=pl.BlockSpec((None, N_MOD_ROWS, 1024), lambda l, j: (l, 0, j)),
        out_shape=jax.ShapeDtypeStruct((DEPTH, N_MOD_ROWS, 6 * D), F32),
        compiler_params=_params(("arbitrary", "arbitrary")),
        name="mod",
    )(cond, w_mod, b_mod.reshape(DEPTH, 1, 6 * D))


def _inproj_kernel(x_ref, mod_ref, wqkv_ref, wkt_ref, wfp_ref, wifh_ref, wifl_ref, bif_ref,
                   qkv_ref, kt_ref, fp_ref, gi_ref, gf_ref):
    u = x_ref[...] * (1.0 + mod_ref[1:2, :]) + mod_ref[0:1, :]
    uh, ul = _split2(u)
    z = _dot(uh, wqkv_ref[...])
    qkv_ref[:, 0:A_W] = z[:, 0:A_W].astype(BF16)
    qkv_ref[:, A_W:2 * A_W] = (z[:, A_W:2 * A_W] * K_SCALE).astype(BF16)
    qkv_ref[:, 2 * A_W:3 * A_W] = z[:, 2 * A_W:3 * A_W].astype(BF16)
    kt_ref[...] = (_dot_nt(wkt_ref[...], uh) * K_SCALE).astype(BF16)
    fp_ref[...] = _dot(uh, wfp_ref[...])
    g = _dot(uh, wifh_ref[...]) + _dot(ul, wifh_ref[...]) + _dot(uh, wifl_ref[...]) + bif_ref[...]
    gi_ref[...] = g[:, 0:128]
    gf_ref[...] = g[:, 128:256]


def _inproj(x, mod_l, row_of_tile, wqkv, wkt, wfp, wifh, wifl, bif):
    n = x.shape[0]
    const = lambda i: (0, 0)
    return pl.pallas_call(
        _inproj_kernel,
        grid=(n // TMM,),
        in_specs=[
            pl.BlockSpec((TMM, D), lambda i: (i, 0)),
            pl.BlockSpec((None, 6, D), lambda i: (row_of_tile(i), 0, 0)),
            pl.BlockSpec((D, 3 * A_W), const),
            pl.BlockSpec((A_W, D), const),
            pl.BlockSpec((D, B_W + C_W), const),
            pl.BlockSpec((D, 256), const),
            pl.BlockSpec((D, 256), const),
            pl.BlockSpec((1, 256), const),
        ],
        out_specs=[
            pl.BlockSpec((TMM, 3 * A_W), lambda i: (i, 0)),
            pl.BlockSpec((A_W, TMM), lambda i: (0, i)),
            pl.BlockSpec((TMM, B_W + C_W), lambda i: (i, 0)),
            pl.BlockSpec((TMM, 128), lambda i: (i, 0)),
            pl.BlockSpec((TMM, 128), lambda i: (i, 0)),
        ],
        out_shape=[
            jax.ShapeDtypeStruct((n, 3 * A_W), BF16),
            jax.ShapeDtypeStruct((A_W, n), BF16),
            jax.ShapeDtypeStruct((n, B_W + C_W), F32),
            jax.ShapeDtypeStruct((n, 128), F32),
            jax.ShapeDtypeStruct((n, 128), F32),
        ],
        compiler_params=_params(("arbitrary",)),
        name="inproj",
    )(x, mod_l, wqkv, wkt, wfp, wifh, wifl, bif)


def _gates_kernel(gi_ref, gf_ref, git_ref, gft_ref, tri_ref, trit_ref,
                  bcol_ref, rcol_ref, rrow_ref, gsum_ref, rmx_ref):
    tri = tri_ref[...]
    trit = trit_ref[...]
    h, m, l = _split3(_log_sigmoid(gf_ref[...]))
    cum_f = _dot(tri, h) + _dot(tri, m) + _dot(tri, l)
    cum_b = _dot(trit, h) + _dot(trit, m) + _dot(trit, l)
    lane = lax.broadcasted_iota(jnp.int32, (CH, 128), 1)
    bcol = jnp.where(lane < HEADS, cum_f, cum_b)
    bcol_ref[...] = bcol
    rcol_ref[...] = gi_ref[...] - bcol

    h, m, l = _split3(_log_sigmoid(gft_ref[...]))
    row_f = _dot(h, trit) + _dot(m, trit) + _dot(l, trit)
    row_b = _dot(h, tri) + _dot(m, tri) + _dot(l, tri)
    row = lax.broadcasted_iota(jnp.int32, (8, CH), 0)
    brow = jnp.where(row < HEADS, row_f, row_b)
    rrow = git_ref[...] - brow
    rrow_ref[...] = rrow
    row1 = lax.broadcasted_iota(jnp.int32, (8, 1), 0)
    total = jnp.where(row1 < HEADS, brow[:, CH - 1:CH], brow[:, 0:1])
    gsum_ref[...] = jnp.broadcast_to(total, (8, 128))
    rmx_ref[...] = jnp.broadcast_to(jnp.max(rrow, axis=1, keepdims=True), (8, 128))


def _gates(gi, gf, git, gft, tri, trit):
    n = gi.shape[0]
    nch = n // CH
    const = lambda i: (0, 0)
    return pl.pallas_call(
        _gates_kernel,
        grid=(nch,),
        in_specs=[
            pl.BlockSpec((CH, 128), lambda i: (i, 0)),
            pl.BlockSpec((CH, 128), lambda i: (i, 0)),
            pl.BlockSpec((8, CH), lambda i: (0, i)),
            pl.BlockSpec((8, CH), lambda i: (0, i)),
            pl.BlockSpec((CH, CH), const),
            pl.BlockSpec((CH, CH), const),
        ],
        out_specs=[
            pl.BlockSpec((CH, 128), lambda i: (i, 0)),
            pl.BlockSpec((CH, 128), lambda i: (i, 0)),
            pl.BlockSpec((8, CH), lambda i: (0, i)),
            pl.BlockSpec((None, 8, 128), lambda i: (i, 0, 0)),
            pl.BlockSpec((None, 8, 128), lambda i: (i, 0, 0)),
        ],
        out_shape=[
            jax.ShapeDtypeStruct((n, 128), F32),
            jax.ShapeDtypeStruct((n, 128), F32),
            jax.ShapeDtypeStruct((8, n), F32),
            jax.ShapeDtypeStruct((nch, 8, 128), F32),
            jax.ShapeDtypeStruct((nch, 8, 128), F32),
        ],
        compiler_params=_params(("arbitrary",)),
        name="gates",
    )(gi, gf, git, gft, tri, trit)


def _mlstm_kernel(*refs, nchunk, has_state):
    if has_state:
        (q_ref, k_ref, v_ref, kt_ref, bcol_ref, rcol_ref, rrow_ref, gsum_ref, rmx_ref, lng_ref,
         cn0_ref, m0_ref, hn_ref, cn_st) = refs
    else:
        (q_ref, k_ref, v_ref, kt_ref, bcol_ref, rcol_ref, rrow_ref, gsum_ref, rmx_ref, lng_ref,
         hn_ref, cn_out_ref, m_out_ref) = refs
    head = pl.program_id(1)
    lane = lax.broadcasted_iota(jnp.int32, (CH, 128), 1)
    tpos = lax.broadcasted_iota(jnp.int32, (CH, CH), 0)
    spos = lax.broadcasted_iota(jnp.int32, (CH, CH), 1)
    masks = (spos <= tpos, spos >= tpos)

    def col(ref, j, idx):
        blk = ref[j * CH:(j + 1) * CH, :]
        return jnp.sum(jnp.where(lane == idx, blk, 0.0), axis=1, keepdims=True)

    m_in = [[None] * nchunk, [None] * nchunk]
    for d in range(2):
        idx = d * HEADS + head
        cn = cn0_ref[d] if has_state else None
        m = m0_ref[pl.ds(idx, 1), 0:1] if has_state else None
        order = range(nchunk) if d == 0 else range(nchunk - 1, -1, -1)
        for pos_in_sweep, j in enumerate(order):
            if m is not None:
                cn_st[d, j] = cn
                m_in[d][j] = m
            if has_state and pos_in_sweep == nchunk - 1:
                break
            total = gsum_ref[j, pl.ds(idx, 1), 0:1]
            rmax = rmx_ref[j, pl.ds(idx, 1), 0:1]
            mm = rmax if m is None else jnp.maximum(m, rmax)
            ws = jnp.exp(col(rcol_ref, j, idx) - mm)
            vj = v_ref[j * CH:(j + 1) * CH, :].astype(F32)
            vs = jnp.concatenate([(vj * ws).astype(BF16),
                                  jnp.where(lane == 0, ws, 0.0).astype(BF16)], axis=1)
            upd = _dot(kt_ref[:, j * CH:(j + 1) * CH], vs)
            cn = upd if m is None else jnp.exp(m - mm) * cn + upd
            m = total + mm
        if not has_state:
            cn_out_ref[d] = cn
            m_out_ref[d:d + 1, :] = jnp.broadcast_to(m, (1, 128))

    for j in range(nchunk):
        sl = slice(j * CH, (j + 1) * CH)
        qj = q_ref[sl, :]
        s_raw = _dot_nt(qj, k_ref[sl, :])
        p_sum, h_inter = None, None
        for d in range(2):
            idx = d * HEADS + head
            rr = rrow_ref[pl.ds(idx, 1), sl]
            m_row = jnp.max(jnp.where(masks[d], rr, -jnp.inf), axis=1, keepdims=True)
            m_prev = m_in[d][j]
            if m_prev is not None:
                m_row = jnp.maximum(m_row, m_prev)
            p = s_raw * jnp.where(masks[d], jnp.exp(rr - m_row), 0.0)
            den = jnp.sum(p, axis=1, keepdims=True)
            if m_prev is not None:
                inter = jnp.exp(m_prev - m_row)
                qc = _dot(qj, cn_st[d, j].astype(BF16))
                den = den + inter * qc[:, DH:DH + 1]
            floor = jnp.exp(-(col(bcol_ref, j, idx) + m_row))
            rinv = 1.0 / jnp.maximum(jnp.abs(den), floor)
            p_sum = p * rinv if p_sum is None else p_sum + p * rinv
            if m_prev is not None:
                hi = (inter * rinv) * qc[:, 0:DH]
                h_inter = hi if h_inter is None else h_inter + hi
        hj = _dot(p_sum.astype(BF16), v_ref[sl, :])
        if h_inter is not None:
            hj = hj + h_inter
        mu = jnp.mean(hj, axis=-1, keepdims=True)
        hc = hj - mu
        var = jnp.mean(hc * hc, axis=-1, keepdims=True)
        hn_ref[sl, :] = hc * lax.rsqrt(var + LN_EPS) * lng_ref[...]


def _mlstm(qkv, kt, bcol, rcol, rrow, gsum, rmx, lng, tok0, nseq, t, cn0=None, m0=None):
    nchunk = t // CH
    sb = tok0 // t
    cb = tok0 // CH
    has_state = cn0 is not None
    in_specs = [
        pl.BlockSpec((t, DH), lambda s, h: (sb + s, h)),
        pl.BlockSpec((t, DH), lambda s, h: (sb + s, HEADS + h)),
        pl.BlockSpec((t, DH), lambda s, h: (sb + s, 2 * HEADS + h)),
        pl.BlockSpec((DH, t), lambda s, h: (h, sb + s)),
        pl.BlockSpec((t, 128), lambda s, h: (sb + s, 0)),
        pl.BlockSpec((t, 128), lambda s, h: (sb + s, 0)),
        pl.BlockSpec((8, t), lambda s, h: (0, sb + s)),
        pl.BlockSpec((nchunk, 8, 128), lambda s, h: (cb // nchunk + s, 0, 0)),
        pl.BlockSpec((nchunk, 8, 128), lambda s, h: (cb // nchunk + s, 0, 0)),
        pl.BlockSpec((1, DH), lambda s, h: (0, h)),
    ]
    args = [qkv, qkv, qkv, kt, bcol, rcol, rrow, gsum, rmx, lng]
    hn_spec = pl.BlockSpec((t, DH), lambda s, h: (s, h))
    hn_shape = jax.ShapeDtypeStruct((nseq * t, A_W), F32)
    if has_state:
        in_specs += [
            pl.BlockSpec((None, 2, None, DH, 2 * DH), lambda s, h: (s, 0, h, 0, 0)),
            pl.BlockSpec((None, 8, 128), lambda s, h: (s, 0, 0)),
        ]
        args += [cn0, m0]
        out_specs, out_shape = hn_spec, hn_shape
        scratch = [pltpu.VMEM((2, nchunk, DH, 2 * DH), F32)]
    else:
        out_specs = [
            hn_spec,
            pl.BlockSpec((None, 2, None, DH, 2 * DH), lambda s, h: (s, 0, h, 0, 0)),
            pl.BlockSpec((None, None, 2, 128), lambda s, h: (s, h, 0, 0)),
        ]
        out_shape = [
            hn_shape,
            jax.ShapeDtypeStruct((nseq, 2, HEADS, DH, 2 * DH), F32),
            jax.ShapeDtypeStruct((nseq, HEADS, 2, 128), F32),
        ]
        scratch = []
    return pl.pallas_call(
        functools.partial(_mlstm_kernel, nchunk=nchunk, has_state=has_state),
        grid=(nseq, HEADS),
        in_specs=in_specs,
        out_specs=out_specs,
        out_shape=out_shape,
        scratch_shapes=scratch,
        compiler_params=_params(("arbitrary", "arbitrary")),
        name="mlstm_lat" if has_state else "mlstm_ctx",
    )(*args)


def _fftpool_kernel(fp_ref, cth_ref, ctl_ref, sth_ref, stl_ref, bcsh_ref, bcsl_ref, icnt_ref,
                    wpool_ref, pscale_ref, yf_ref, pp_ref, pad_ref, *, t):
    fh, fl = _split2(fp_ref[:, 0:B_W])
    xcs = _dot(fh, bcsh_ref[...]) + _dot(fl, bcsh_ref[...]) + _dot(fh, bcsl_ref[...])
    xch, xcl = _split2(xcs[:, 0:B_W])
    xsh, xsl = _split2(xcs[:, B_W:2 * B_W])
    y = (_dot(cth_ref[...], xch) + _dot(ctl_ref[...], xch) + _dot(cth_ref[...], xcl)
         - _dot(sth_ref[...], xsh) - _dot(stl_ref[...], xsh) - _dot(sth_ref[...], xsl))
    yf_ref[...] = (y * ((t * B_GW) ** -0.5)).astype(BF16)

    p = fp_ref[:, B_W:B_W + C_W]
    zero = jnp.zeros((POOL_HALO, C_W), F32)
    pad_ref[0:POOL_HALO, :] = zero
    pad_ref[POOL_HALO + t:2 * POOL_HALO + t, :] = zero
    pad_ref[POOL_HALO:POOL_HALO + t, :] = p

    def sh(dlt):
        return pad_ref[POOL_HALO + dlt:POOL_HALO + dlt + t, :]

    w2 = sh(-1) + p
    w4 = w2 + sh(-2) + sh(1)
    w8 = w4 + sh(-4) + sh(-3) + sh(2) + sh(3)
    w16 = w8 + sh(-8) + sh(-7) + sh(-6) + sh(-5) + sh(4) + sh(5) + sh(6) + sh(7)
    grp = lax.broadcasted_iota(jnp.int32, (t, C_W), 1) // C_GW
    wsum = jnp.where(grp == 0, w2, jnp.where(grp == 1, w4, jnp.where(grp == 2, w8, w16)))
    pooled = wsum * icnt_ref[...] - p
    pp_ref[...] = (_dot(pooled.astype(BF16), wpool_ref[...]) * pscale_ref[...]).astype(BF16)


def _fftpool(fp, consts, wpool_bd, pscale, tok0, nseq, t):
    cth, ctl, sth, stl, bcsh, bcsl, icnt = consts
    sb = tok0 // t
    const = lambda s: (0, 0)
    return pl.pallas_call(
        functools.partial(_fftpool_kernel, t=t),
        grid=(nseq,),
        in_specs=[
            pl.BlockSpec((t, B_W + C_W), lambda s: (sb + s, 0)),
            pl.BlockSpec((t, t), const), pl.BlockSpec((t, t), const),
            pl.BlockSpec((t, t), const), pl.BlockSpec((t, t), const),
            pl.BlockSpec((B_W, 2 * B_W), const), pl.BlockSpec((B_W, 2 * B_W), const),
            pl.BlockSpec((t, C_W), const),
            pl.BlockSpec((C_W, C_W), const),
            pl.BlockSpec((1, C_W), const),
        ],
        out_specs=[pl.BlockSpec((t, B_W), lambda s: (s, 0)), pl.BlockSpec((t, C_W), lambda s: (s, 0))],
        out_shape=[jax.ShapeDtypeStruct((nseq * t, B_W), BF16), jax.ShapeDtypeStruct((nseq * t, C_W), BF16)],
        scratch_shapes=[pltpu.VMEM((t + 2 * POOL_HALO, C_W), F32)],
        compiler_params=_params(("arbitrary",)),
        name=f"fftpool_{t}",
    )(fp, cth, ctl, sth, stl, bcsh, bcsl, icnt, wpool_bd, pscale)


def _merge_kernel(x_ref, mod_ref, hnc_ref, hnl_ref, yfc_ref, yfl_ref, ppc_ref, ppl_ref, wo_ref, wg_ref, wpa_ref,
                  wpb_ref, wpc_ref, wout_ref, g1_ref, b1_ref, wrh_ref, wrl_ref, br_ref, x1_ref, u2_ref, route_ref,
                  *, n_ctx_tiles):
    is_ctx = pl.program_id(0) < n_ctx_tiles
    hn = jnp.where(is_ctx, hnc_ref[...], hnl_ref[...])
    yf = jnp.where(is_ctx, yfc_ref[...], yfl_ref[...])
    pp = jnp.where(is_ctx, ppc_ref[...], ppl_ref[...])
    x = x_ref[...]
    ub = (x * (1.0 + mod_ref[1:2, :]) + mod_ref[0:1, :]).astype(BF16)
    a = (hn * _sigmoid(_dot(ub, wo_ref[...]))).astype(BF16)
    acc = _sigmoid(_dot(ub, wg_ref[:, 0:D])) * _dot(a, wpa_ref[...])
    acc = acc + _sigmoid(_dot(ub, wg_ref[:, D:2 * D])) * _dot(yf, wpb_ref[...])
    acc = acc + _sigmoid(_dot(ub, wg_ref[:, 2 * D:3 * D])) * _dot(pp, wpc_ref[...])
    out = _dot(acc.astype(BF16), wout_ref[...])
    x1 = _layer_norm(ALPHA * x + mod_ref[2:3, :] * out, g1_ref[...], b1_ref[...])
    x1_ref[...] = x1
    u2 = x1 * (1.0 + mod_ref[4:5, :]) + mod_ref[3:4, :]
    u2_ref[...] = u2.astype(BF16)

    uh, ul = _split2(u2)
    lt = _dot_nt(wrh_ref[...], uh) + _dot_nt(wrh_ref[...], ul) + _dot_nt(wrl_ref[...], uh) + br_ref[:, 0:1]
    gl = [lt[g:g + 1, :] for g in range(N_GROUPS)]
    gmax = jnp.maximum(jnp.maximum(gl[0], gl[1]), jnp.maximum(gl[2], gl[3]))
    gidx = jnp.where(gl[0] >= gmax, 0, jnp.where(gl[1] >= gmax, 1, jnp.where(gl[2] >= gmax, 2, 3)))
    gval = 1.0 / (jnp.exp(gl[0] - gmax) + jnp.exp(gl[1] - gmax) + jnp.exp(gl[2] - gmax) + jnp.exp(gl[3] - gmax))
    fe = []
    for e in range(EXP_PER_GROUP):
        r = [lt[N_GROUPS + g * EXP_PER_GROUP + e:N_GROUPS + g * EXP_PER_GROUP + e + 1, :] for g in range(N_GROUPS)]
        fe.append(jnp.where(gidx == 0, r[0], jnp.where(gidx == 1, r[1], jnp.where(gidx == 2, r[2], r[3]))))

    def top1(v):
        vmax = jnp.maximum(jnp.maximum(v[0], v[1]), jnp.maximum(v[2], v[3]))
        imax = jnp.where(v[0] >= vmax, 0, jnp.where(v[1] >= vmax, 1, jnp.where(v[2] >= vmax, 2, 3)))
        return vmax, imax

    v1, i1 = top1(fe)
    v2, i2 = top1([jnp.where(i1 == e, -jnp.inf, fe[e]) for e in range(EXP_PER_GROUP)])
    tt = jnp.exp(v2 - v1)
    w1 = gval / (1.0 + tt)
    w2 = gval * tt / (1.0 + tt)
    e1 = (gidx * EXP_PER_GROUP + i1).astype(F32)
    e2 = (gidx * EXP_PER_GROUP + i2).astype(F32)
    route_ref[...] = jnp.zeros(route_ref.shape, F32)
    route_ref[0:1, :] = e1
    route_ref[1:2, :] = e2
    route_ref[2:3, :] = w1
    route_ref[3:4, :] = w2


def _merge(x, mod_l, row_of_tile, hn_cl, yf_cl, pp_cl, wo, wg, wpa, wpb, wpc, wout, g1, b1, wrh, wrl, br):
    n = x.shape[0]
    nct = hn_cl[0].shape[0] // TMM
    const = lambda i: (0, 0)
    tok = lambda i: (i, 0)
    ctx = lambda i: (jnp.minimum(i, nct - 1), 0)
    lat = lambda i: (jnp.maximum(i - nct, 0), 0)
    return pl.pallas_call(
        functools.partial(_merge_kernel, n_ctx_tiles=nct),
        grid=(n // TMM,),
        in_specs=[
            pl.BlockSpec((TMM, D), tok),
            pl.BlockSpec((None, 6, D), lambda i: (row_of_tile(i), 0, 0)),
            pl.BlockSpec((TMM, A_W), ctx), pl.BlockSpec((TMM, A_W), lat),
            pl.BlockSpec((TMM, B_W), ctx), pl.BlockSpec((TMM, B_W), lat),
            pl.BlockSpec((TMM, C_W), ctx), pl.BlockSpec((TMM, C_W), lat),
            pl.BlockSpec((D, A_W), const),
            pl.BlockSpec((D, 3 * D), const),
            pl.BlockSpec((A_W, D), const),
            pl.BlockSpec((B_W, D), const),
            pl.BlockSpec((C_W, D), const),
            pl.BlockSpec((D, D), const),
            pl.BlockSpec((1, D), const),
            pl.BlockSpec((1, D), const),
            pl.BlockSpec((32, D), const),
            pl.BlockSpec((32, D), const),
            pl.BlockSpec((32, 128), const),
        ],
        out_specs=[pl.BlockSpec((TMM, D), tok), pl.BlockSpec((TMM, D), tok), pl.BlockSpec((8, TMM), lambda i: (0, i))],
        out_shape=[jax.ShapeDtypeStruct((n, D), F32), jax.ShapeDtypeStruct((n, D), BF16),
                   jax.ShapeDtypeStruct((8, n), F32)],
        compiler_params=_params(("arbitrary",)),
        name="merge",
    )(x, mod_l, *hn_cl, *yf_cl, *pp_cl, wo, wg, wpa, wpb, wpc, wout, g1, b1, wrh, wrl, br)


def _lsort_kernel(route_ref, u2_ref, upper_ref, lower_ref, xl_ref, cnt_ref, loc_ref):
    e0 = route_ref[0:1, :].astype(jnp.int32)
    e1 = route_ref[1:2, :].astype(jnp.int32)
    eid = lax.broadcasted_iota(jnp.int32, (EP, TM), 0)
    m0 = eid == e0
    m1 = eid == e1
    member = jnp.where(m0 | m1, 1.0, 0.0)
    rank = _dot(member.astype(BF16), upper_ref[...])
    cnt = jnp.sum(member, axis=1, keepdims=True)
    nwin = jnp.floor((cnt + (W_ROWS - 1)) * (1.0 / W_ROWS))
    first = _dot(lower_ref[...], jnp.broadcast_to(nwin, (EP, 128)).astype(BF16))[:, 0:1] * W_ROWS
    r0 = jnp.sum(jnp.where(m0, first + rank, 0.0), axis=0, keepdims=True).astype(jnp.int32)
    r1 = jnp.sum(jnp.where(m1, first + rank, 0.0), axis=0, keepdims=True).astype(jnp.int32)
    rid = lax.broadcasted_iota(jnp.int32, (RL, TM), 0)
    h0 = rid == r0
    h1 = rid == r1
    xl_ref[:, 0:D] = _dot(jnp.where(h0 | h1, 1.0, 0.0).astype(BF16), u2_ref[...])
    wsel = jnp.where(h0, route_ref[2:3, :], 0.0) + jnp.where(h1, route_ref[3:4, :], 0.0)
    xl_ref[:, D:XW] = jnp.broadcast_to(jnp.sum(wsel, axis=1, keepdims=True), (RL, XW - D))
    cnt_ref[...] = jnp.broadcast_to(cnt[0:N_EXPERTS, :], (N_EXPERTS, 128))
    loc_ref[...] = jnp.zeros(loc_ref.shape, F32)
    loc_ref[0:1, :] = r0.astype(F32)
    loc_ref[1:2, :] = r1.astype(F32)


def _lsort(route, u2, upper, lower):
    n = u2.shape[0]
    nb = n // TM
    const = lambda i: (0, 0)
    return pl.pallas_call(
        _lsort_kernel,
        grid=(nb,),
        in_specs=[
            pl.BlockSpec((8, TM), lambda i: (0, i)),
            pl.BlockSpec((TM, D), lambda i: (i, 0)),
            pl.BlockSpec((TM, TM), const),
            pl.BlockSpec((EP, EP), const),
        ],
        out_specs=[
            pl.BlockSpec((RL, XW), lambda i: (i, 0)),
            pl.BlockSpec((None, N_EXPERTS, 128), lambda i: (i, 0, 0)),
            pl.BlockSpec((8, TM), lambda i: (0, i)),
        ],
        out_shape=[
            jax.ShapeDtypeStruct((nb * RL, XW), F32),
            jax.ShapeDtypeStruct((nb, N_EXPERTS, 128), F32),
            jax.ShapeDtypeStruct((8, n), F32),
        ],
        compiler_params=_params(("arbitrary",)),
        name="lsort",
    )(route, u2, upper, lower)


def _fetch_windows(src_hbm, rows_ref, first, n_windows, dst, sem):
    def body(w, c):
        src = pl.ds(pl.multiple_of(rows_ref[first + w], W_ROWS), W_ROWS)
        pltpu.make_async_copy(src_hbm.at[src], dst.at[pl.ds(pl.multiple_of(w * W_ROWS, W_ROWS), W_ROWS)], sem).start()
        return c

    lax.fori_loop(0, n_windows, body, 0, unroll=8)


def _wait_windows(src_hbm, dst, sem):
    pltpu.make_async_copy(src_hbm.at[pl.ds(0, dst.shape[0])], dst, sem).wait()


def _experts_kernel(wsrc_ref, te_ref, nused_ref, xl_hbm, wup_ref, wdn_ref, ys_ref, xbuf, sems):
    j = pl.program_id(0)
    nused = nused_ref[0]
    slot = j % 2

    @pl.when(j == 0)
    def _():
        _fetch_windows(xl_hbm, wsrc_ref, 0, WPT, xbuf.at[0], sems.at[0])

    @pl.when(j + 1 < nused)
    def _():
        _fetch_windows(xl_hbm, wsrc_ref, (j + 1) * WPT, WPT, xbuf.at[1 - slot], sems.at[1 - slot])

    @pl.when(j < nused)
    def _():
        _wait_windows(xl_hbm, xbuf.at[slot], sems.at[slot])
        xw = xbuf[slot]
        hid = _dot(xw[:, 0:D].astype(BF16), wup_ref[...].astype(BF16))
        a = hid[:, 0:D_EXPERT]
        act = a * _sigmoid(a) * hid[:, D_EXPERT:2 * D_EXPERT] * xw[:, D:D + 1]
        ys_ref[...] = _dot(act.astype(BF16), wdn_ref[...].astype(BF16))

    @pl.when(j >= nused)
    def _():
        ys_ref[...] = jnp.zeros(ys_ref.shape, F32)


def _experts(wsrc, tile_expert, nused, xl, w_up, w_down, layer, nt):
    return pl.pallas_call(
        _experts_kernel,
        grid_spec=pltpu.PrefetchScalarGridSpec(
            num_scalar_prefetch=3,
            grid=(nt,),
            in_specs=[
                pl.BlockSpec(memory_space=pl.ANY),
                pl.BlockSpec((None, None, D, 2 * D_EXPERT), lambda i, ws, te, nu: (layer, te[i], 0, 0)),
                pl.BlockSpec((None, None, D_EXPERT, D), lambda i, ws, te, nu: (layer, te[i], 0, 0)),
            ],
            out_specs=pl.BlockSpec((TE, D), lambda i, ws, te, nu: (i, 0)),
            scratch_shapes=[pltpu.VMEM((2, TE, XW), F32), pltpu.SemaphoreType.DMA((2,))],
        ),
        out_shape=jax.ShapeDtypeStruct((nt * TE, D), F32),
        compiler_params=_params(("arbitrary",)),
        name="experts",
    )(wsrc, tile_expert, nused, xl, w_up, w_down)


def _combine_kernel(gw_ref, ys_hbm, x1_ref, mod_ref, loc_ref, g2_ref, b2_ref, o_ref, ybuf, sems):
    i = pl.program_id(0)
    slot = i % 2

    @pl.when(i == 0)
    def _():
        _fetch_windows(ys_hbm, gw_ref, 0, NSLOT, ybuf.at[0], sems.at[0])

    @pl.when(i + 1 < pl.num_programs(0))
    def _():
        _fetch_windows(ys_hbm, gw_ref, (i + 1) * NSLOT, NSLOT, ybuf.at[1 - slot], sems.at[1 - slot])

    _wait_windows(ys_hbm, ybuf.at[slot], sems.at[slot])
    r0 = loc_ref[:, 0:1].astype(jnp.int32)
    r1 = loc_ref[:, 1:2].astype(jnp.int32)
    lid = lax.broadcasted_iota(jnp.int32, (TM, RL), 1)
    pt = jnp.where((lid == r0) | (lid == r1), 1.0, 0.0).astype(BF16)
    yh, yl = _split2(ybuf[slot])
    y = _dot(pt, yh) + _dot(pt, yl)
    o_ref[...] = _layer_norm(ALPHA * x1_ref[...] + mod_ref[5:6, :] * y, g2_ref[...], b2_ref[...])


def _combine(gw, ys, x1, mod_l, row_of_tile, loc_t, g2, b2):
    n = x1.shape[0]
    return pl.pallas_call(
        _combine_kernel,
        grid_spec=pltpu.PrefetchScalarGridSpec(
            num_scalar_prefetch=1,
            grid=(n // TM,),
            in_specs=[
                pl.BlockSpec(memory_space=pl.ANY),
                pl.BlockSpec((TM, D), lambda i, gw: (i, 0)),
                pl.BlockSpec((None, 6, D), lambda i, gw: (row_of_tile(i), 0, 0)),
                pl.BlockSpec((TM, 8), lambda i, gw: (i, 0)),
                pl.BlockSpec((1, D), lambda i, gw: (0, 0)),
                pl.BlockSpec((1, D), lambda i, gw: (0, 0)),
            ],
            out_specs=pl.BlockSpec((TM, D), lambda i, gw: (i, 0)),
            scratch_shapes=[pltpu.VMEM((2, RL, D), F32), pltpu.SemaphoreType.DMA((2,))],
        ),
        out_shape=jax.ShapeDtypeStruct((n, D), F32),
        compiler_params=_params(("arbitrary",)),
        name="combine",
    )(gw, ys, x1, mod_l, loc_t, g2, b2)


def _routing_tables(cnt, nt):
    nb = cnt.shape[0]
    nwin = (cnt + W_ROWS - 1) // W_ROWS
    lo_inc = jnp.cumsum(nwin, axis=1)
    lo = lo_inc - nwin
    bo_inc = jnp.cumsum(nwin, axis=0)
    bo = bo_inc - nwin
    we = bo_inc[-1]
    wpad = ((we + WPT - 1) // WPT) * WPT
    eend = jnp.cumsum(wpad)
    eoff = eend - wpad
    nused = (eend[-1] // WPT).reshape(1)
    tile_expert = jnp.minimum(
        jnp.sum((jnp.arange(nt, dtype=jnp.int32)[:, None] * WPT >= eend[None, :]).astype(jnp.int32), axis=1),
        N_EXPERTS - 1)
    experts = jnp.arange(N_EXPERTS, dtype=jnp.int32)
    pick = lambda onehot, table: jnp.sum(jnp.where(onehot, table, 0), axis=-1)
    g = jnp.arange(nt * WPT, dtype=jnp.int32)
    oh_e = jnp.repeat(tile_expert[:, None] == experts[None, :], WPT, axis=0)
    ig = g - pick(oh_e, eoff[None, :])
    binc_g = pick(oh_e[:, None, :], bo_inc[None, :, :])
    bg = jnp.minimum(jnp.sum((binc_g <= ig[:, None]).astype(jnp.int32), axis=1), nb - 1)
    oh_b = bg[:, None] == jnp.arange(nb, dtype=jnp.int32)[None, :]
    shift_g = pick(oh_b, pick(oh_e[:, None, :], (lo - bo)[None, :, :]))
    empty_window_row = RL - W_ROWS
    wsrc = jnp.where(ig < pick(oh_e, we[None, :]), bg * RL + (shift_g + ig) * W_ROWS, empty_window_row)
    s = jnp.arange(NSLOT, dtype=jnp.int32)
    es = jnp.minimum(jnp.sum((lo_inc[:, None, :] <= s[None, :, None]).astype(jnp.int32), axis=2), N_EXPERTS - 1)
    oh_s = es[:, :, None] == experts[None, None, :]
    gwin = pick(oh_s, (eoff[None, :] + bo - lo)[:, None, :]) + s[None, :]
    gw = jnp.where(s[None, :] < lo_inc[:, -1:], gwin, 0) * W_ROWS
    i32 = lambda a: a.astype(jnp.int32)
    return i32(wsrc), i32(tile_expert), i32(nused), i32(gw).reshape(-1)


def _grid_pos_embed(n_tok):
    rows = n_tok // GRID_W
    r, col = np.meshgrid(np.arange(rows, dtype=np.float32), np.arange(GRID_W, dtype=np.float32), indexing="ij")
    r, col = r.reshape(-1), col.reshape(-1)
    quarter = D // 4
    freq = (1.0 / (np.float32(POS_BASE) ** (np.arange(quarter, dtype=np.float32) / np.float32(quarter)))).astype(np.float32)
    ang_r = r[:, None] * freq[None, :]
    ang_c = col[:, None] * freq[None, :]
    return jnp.asarray(np.concatenate([np.sin(ang_r), np.cos(ang_r), np.sin(ang_c), np.cos(ang_c)], axis=-1),
                       dtype=F32)


def _dft_consts(t):
    def hl(a):
        hi = a.astype(BF16)
        lo = (a - hi.astype(np.float64)).astype(BF16)
        return jnp.asarray(hi), jnp.asarray(lo)

    tt = np.arange(t)
    ang = 2.0 * np.pi * ((tt[:, None] * tt[None, :]) % t) / t
    cth, ctl = hl(np.cos(ang))
    sth, stl = hl(np.sin(ang))
    cc = np.arange(B_GW)
    angc = 2.0 * np.pi * ((cc[:, None] * cc[None, :]) % B_GW) / B_GW
    eye = np.eye(B_W // B_GW)
    bcs = np.concatenate([np.kron(eye, np.cos(angc)), np.kron(eye, np.sin(angc))], axis=1)
    bcsh, bcsl = hl(bcs)
    pos = np.arange(t)
    icnt = np.concatenate(
        [np.repeat((1.0 / (np.minimum(pos + w // 2, t) - np.maximum(pos - w // 2, 0)))[:, None], C_GW, axis=1)
         for w in POOL_WINDOWS], axis=1)
    return cth, ctl, sth, stl, bcsh, bcsl, jnp.asarray(icnt, F32)


def kernel(x_prompt, x_sample, c, state_C, state_n, state_m, c_ctx, ln_in_g, ln_in_b, w_mod, b_mod, w_in, b_if, ln_a_g, w_pa, w_pb, w_pc, w_pool, pool_scale, w_out, ln1_g, ln1_b, w_r1, b_r1, w_r2, b_r2, w_up, w_down, ln2_g, ln2_b):
    bc, tc, _ = x_prompt.shape
    bl, tl, _ = x_sample.shape
    nc, nl = bc * tc, bl * tl
    n = nc + nl
    assert tc == CH and tl % CH == 0 and nc % tl == 0 and bl + 1 <= N_MOD_ROWS
    nt = -(-(2 * n // W_ROWS + (n // TM) * N_EXPERTS + N_EXPERTS * (WPT - 1)) // WPT)
    upper = jnp.asarray(np.triu(np.ones((TM, TM), np.float32), 1), BF16)
    lower = jnp.asarray(np.tril(np.ones((EP, EP), np.float32), -1), BF16)

    def row_fn(tile):
        return lambda i: jnp.where(i < nc // tile, 0, 1 + jnp.maximum(i - nc // tile, 0) // (tl // tile))

    row_of_tile, row_of_proj_tile = row_fn(TM), row_fn(TMM)

    wqkv = w_in[:, :, 0:3 * A_W].astype(BF16)
    wkt = jnp.swapaxes(w_in[:, :, A_W:2 * A_W], 1, 2).astype(BF16)
    wo = w_in[:, :, 3 * A_W:4 * A_W].astype(BF16)
    c0 = 4 * A_W
    wif = w_in[:, :, c0:c0 + 16].reshape(DEPTH, D, 2, 2, HEADS)
    zpad = jnp.zeros((DEPTH, D, 128 - 2 * HEADS), F32)
    wif = jnp.concatenate([wif[:, :, :, 0, :].reshape(DEPTH, D, 2 * HEADS), zpad,
                           wif[:, :, :, 1, :].reshape(DEPTH, D, 2 * HEADS), zpad], axis=2)
    wifh, wifl = _split2(wif)
    bpad = jnp.zeros((DEPTH, 128 - 2 * HEADS), F32)
    bif = jnp.concatenate([b_if[:, :, 0, :].reshape(DEPTH, 2 * HEADS), bpad,
                           b_if[:, :, 1, :].reshape(DEPTH, 2 * HEADS), bpad], axis=1).reshape(DEPTH, 1, 256)
    wfp = w_in[:, :, c0 + 16:c0 + 16 + B_W + C_W].astype(BF16)
    wg = w_in[:, :, c0 + 16 + B_W + C_W:].astype(BF16)
    wpa, wpb, wpc, wout = (w.astype(BF16) for w in (w_pa, w_pb, w_pc, w_out))
    eye = jnp.eye(C_W // C_GW, dtype=F32)
    wpool_bd = jnp.einsum("gh,lgce->lgche", eye, w_pool).reshape(DEPTH, C_W, C_W).astype(BF16)
    wr = jnp.concatenate([w_r1, w_r2.reshape(DEPTH, D, N_EXPERTS),
                          jnp.zeros((DEPTH, D, 32 - N_GROUPS - N_EXPERTS), F32)], axis=2)
    wrh, wrl = _split2(jnp.swapaxes(wr, 1, 2))
    br = jnp.concatenate([b_r1, b_r2.reshape(DEPTH, N_EXPERTS),
                          jnp.zeros((DEPTH, 32 - N_GROUPS - N_EXPERTS), F32)], axis=1)
    br = jnp.broadcast_to(br[:, :, None], (DEPTH, 32, 128))

    tri_np = np.tril(np.ones((CH, CH), np.float32))
    tri, trit = jnp.asarray(tri_np, BF16), jnp.asarray(tri_np.T, BF16)
    consts_c, consts_l = _dft_consts(tc), _dft_consts(tl)

    cond = jnp.concatenate([c_ctx[None, :], c, jnp.zeros((N_MOD_ROWS - 1 - bl, D), F32)], axis=0)
    mod = _mod_all(cond, w_mod, b_mod).reshape(DEPTH, N_MOD_ROWS, 6, D)

    x = _ln_in(x_prompt.reshape(nc, D), x_sample.reshape(nl, D), _grid_pos_embed(tl),
               ln_in_g.reshape(1, D), ln_in_b.reshape(1, D), tl)

    cn0 = jnp.concatenate([state_C, state_n[..., None], jnp.zeros(state_C.shape[:-1] + (DH - 1,), F32)], axis=-1)
    m0 = jnp.broadcast_to(state_m.reshape(bl, DEPTH, 2 * HEADS, 1), (bl, DEPTH, 2 * HEADS, 128))

    new_c, new_n, new_m = [], [], []
    for l in range(DEPTH):
        mod_l = mod[l]
        qkv, kt, fp, gi, gf = _inproj(x, mod_l, row_of_proj_tile, wqkv[l], wkt[l], wfp[l], wifh[l], wifl[l], bif[l])
        bcol, rcol, rrow, gsum, rmx = _gates(gi, gf, gi[:, 0:8].T, gf[:, 0:8].T, tri, trit)
        lng = ln_a_g[l].reshape(1, A_W)
        hn_c, cn_c, m_c = _mlstm(qkv, kt, bcol, rcol, rrow, gsum, rmx, lng, 0, bc, tc)
        hn_l = _mlstm(qkv, kt, bcol, rcol, rrow, gsum, rmx, lng, nc, bl, tl, cn0[:, l], m0[:, l])
        pscale = pool_scale[l].reshape(1, C_W)
        yf_c, pp_c = _fftpool(fp, consts_c, wpool_bd[l], pscale, 0, bc, tc)
        yf_l, pp_l = _fftpool(fp, consts_l, wpool_bd[l], pscale, nc, bl, tl)
        x1, u2, route = _merge(x, mod_l, row_of_proj_tile, (hn_c, hn_l), (yf_c, yf_l), (pp_c, pp_l), wo[l], wg[l], wpa[l], wpb[l], wpc[l], wout[l],
                               ln1_g[l].reshape(1, D), ln1_b[l].reshape(1, D), wrh[l], wrl[l], br[l])
        xl, cnt, loc = _lsort(route, u2, upper, lower)
        wsrc, tile_expert, nused, gw = _routing_tables(cnt[:, :, 0].astype(jnp.int32), nt)
        ys = _experts(wsrc, tile_expert, nused, xl, w_up, w_down, l, nt)
        x = _combine(gw, ys, x1, mod_l, row_of_tile, loc.T, ln2_g[l].reshape(1, D), ln2_b[l].reshape(1, D))
        new_c.append(cn_c[..., 0:DH])
        new_n.append(cn_c[..., DH])
        new_m.append(jnp.swapaxes(m_c[..., 0], 1, 2))

    return (x[0:nc].reshape(bc, tc, D), x[nc:].reshape(bl, tl, D),
            jnp.stack(new_c, axis=1), jnp.stack(new_n, axis=1), jnp.stack(new_m, axis=1))
```
